```python
import jax, jax.numpy as jnp
from jax import lax
import numpy as np

D_MODEL = 4096
BATCH = 4
SEQ = 4096
DEPTH = 1

CHUNK = 64
POOL_WINDOWS = (2, 4, 8, 16)
POOL_GROUPS = 4
POOL_WIDTH = D_MODEL // 2
POOL_GROUP_DIM = POOL_WIDTH // POOL_GROUPS
POOL_OUT_DIM = D_MODEL // POOL_GROUPS
SGU_WIDTH = D_MODEL // 2
SGU_HEADS = 8
SGU_HEAD_DIM = SGU_WIDTH // SGU_HEADS
SGU_BLOCK = 128
IN_COLS = POOL_WIDTH + 2 * SGU_WIDTH + 2 * D_MODEL
N_EXPERTS = 32
TOP_K = 4
D_FF_EXPERT = D_MODEL // 4
SWIGLU_LIMIT = 7.0
SWIGLU_ALPHA = 1.702
MOE_BLOCK = 256
EPS = 1e-5

kernel_name = "hybrid_pool_sgu_moe_block"


def rmsnorm(x, g):
    xf = x.astype(jnp.float32)
    r = xf * lax.rsqrt(jnp.mean(xf * xf, axis=-1, keepdims=True) + EPS)
    return (r * g.astype(jnp.float32)).astype(x.dtype)


def multiscale_causal_pool(p):
    S = p.shape[1]
    pf = p.astype(jnp.float32)
    c = jnp.cumsum(pf, axis=1)
    t = jnp.arange(1, S + 1, dtype=jnp.float32)
    outs = []
    for g, w in enumerate(POOL_WINDOWS):
        cg = c[:, :, g]
        lag = jnp.pad(cg, ((0, 0), (w, 0), (0, 0)))[:, :S]
        mean = (cg - lag) / jnp.minimum(t, float(w))[None, :, None]
        outs.append(mean - pf[:, :, g])
    return jnp.stack(outs, axis=2).astype(p.dtype)


def spatial_gating(u, v, sgu_norm_g, w_spatial, b_spatial):
    B, S, _ = v.shape
    vh = rmsnorm(v.reshape(B, S, SGU_HEADS, SGU_HEAD_DIM), sgu_norm_g)
    vc = vh.reshape(B, S // SGU_BLOCK, SGU_BLOCK, SGU_HEADS, SGU_HEAD_DIM)
    pos = jnp.arange(SGU_BLOCK)
    mask = (pos[None, :] // CHUNK) <= (pos[:, None] // CHUNK)
    ws = jnp.where(mask[None], w_spatial, jnp.zeros_like(w_spatial))
    mixed = jnp.einsum('hij,bcjhd->bcihd', ws, vc)
    mixed = mixed + jnp.transpose(b_spatial)[None, None, :, :, None]
    return u * mixed.reshape(B, S, SGU_WIDTH)


def expert_ffn(xb, wg, bg, wu, bu, wd, bd):
    gate = xb @ wg + bg
    up = xb @ wu + bu
    gate = jnp.minimum(gate, SWIGLU_LIMIT)
    up = jnp.clip(up, -SWIGLU_LIMIT, SWIGLU_LIMIT)
    glu = gate * jax.nn.sigmoid(SWIGLU_ALPHA * gate)
    return ((up + 1.0) * glu) @ wd + bd


def moe(h, w_router, b_router, w_gate, b_gate, w_up, b_up, w_down, b_down):
    B, S, D = h.shape
    hf = h.reshape(-1, D)
    N = hf.shape[0]
    logits = (hf @ w_router + b_router).astype(jnp.float32)
    top_vals, top_idx = lax.top_k(logits, TOP_K)
    top_w = jax.nn.softmax(top_vals, axis=-1).astype(h.dtype)
    NK = N * TOP_K
    flat_e = top_idx.reshape(-1).astype(jnp.int32)
    flat_tok = jnp.arange(NK, dtype=jnp.int32) // TOP_K
    order = jnp.argsort(flat_e)
    sorted_e = flat_e[order]
    counts = jnp.bincount(flat_e, length=N_EXPERTS).astype(jnp.int32)
    padded = ((counts + MOE_BLOCK - 1) // MOE_BLOCK) * MOE_BLOCK
    start = jnp.cumsum(counts) - counts
    pend = jnp.cumsum(padded)
    pstart = pend - padded
    rank = jnp.arange(NK, dtype=jnp.int32) - start[sorted_e]
    dest = pstart[sorted_e] + rank
    n_blocks = -(-NK // MOE_BLOCK) + N_EXPERTS
    P = n_blocks * MOE_BLOCK
    row_tok = jnp.zeros((P,), jnp.int32).at[dest].set(flat_tok[order])
    row_w = jnp.zeros((P,), h.dtype).at[dest].set(top_w.reshape(-1)[order])
    block_start = jnp.arange(n_blocks, dtype=jnp.int32) * MOE_BLOCK
    block_e = jnp.minimum(jnp.searchsorted(pend, block_start, side='right'),
                          N_EXPERTS - 1).astype(jnp.int32)

    def run_block(args):
        tok, e = args
        return expert_ffn(hf[tok], w_gate[e], b_gate[e], w_up[e], b_up[e],
                          w_down[e], b_down[e])

    y = lax.map(run_block, (row_tok.reshape(n_blocks, MOE_BLOCK), block_e))
    y = y.reshape(P, D) * row_w[:, None]
    out = jax.ops.segment_sum(y, row_tok, num_segments=N)
    return out.reshape(B, S, D)


def setup_inputs(seed: int = 0) -> dict:
    key = jax.random.key(seed)
    k = jax.random.split(key, 24)
    f32 = jnp.float32
    nrm = lambda kk, shape, s: jax.random.normal(kk, shape, f32) * s
    D, F, E = D_MODEL, D_FF_EXPERT, N_EXPERTS
    return {
        "x": nrm(k[0], (BATCH, SEQ, D), 1.0),
        "norm1_g": 1.0 + nrm(k[1], (D,), 0.02),
        "w_in": nrm(k[2], (D, IN_COLS), D ** -0.5),
        "b_gates": nrm(k[3], (2 * D,), 0.02),
        "w_pool_out": nrm(k[4], (POOL_GROUPS, POOL_GROUP_DIM, POOL_OUT_DIM), POOL_GROUP_DIM ** -0.5),
        "pool_scale": 1.0 + nrm(k[5], (D,), 0.02),
        "sgu_norm_g": 1.0 + nrm(k[6], (SGU_HEADS, SGU_HEAD_DIM), 0.02),
        "w_spatial": nrm(k[7], (SGU_HEADS, SGU_BLOCK, SGU_BLOCK), SGU_BLOCK ** -0.5),
        "b_spatial": 1.0 + nrm(k[8], (SGU_HEADS, SGU_BLOCK), 0.02),
        "w_sgu_out": nrm(k[9], (SGU_WIDTH, D), SGU_WIDTH ** -0.5),
        "w_out": nrm(k[10], (D, D), D ** -0.5),
        "norm2_g": 1.0 + nrm(k[11], (D,), 0.02),
        "w_router": nrm(k[12], (D, E), D ** -0.5),
        "b_router": nrm(k[13], (E,), 0.01),
        "w_gate": nrm(k[14], (E, D, F), D ** -0.5),
        "b_gate": nrm(k[15], (E, F), 0.01),
        "w_up": nrm(k[16], (E, D, F), D ** -0.5),
        "b_up": nrm(k[17], (E, F), 0.01),
        "w_down": nrm(k[18], (E, F, D), F ** -0.5),
        "b_down": nrm(k[19], (E, D), 0.01),
        "normf_g": 1.0 + nrm(k[20], (D,), 0.02),
    }


def reference(x, norm1_g, w_in, b_gates, w_pool_out, pool_scale, sgu_norm_g,
              w_spatial, b_spatial, w_sgu_out, w_out, norm2_g, w_router, b_router,
              w_gate, b_gate, w_up, b_up, w_down, b_down, normf_g):
    B, S, D = x.shape
    for _ in range(DEPTH):
        h = rmsnorm(x, norm1_g)
        z = h @ w_in
        o1 = POOL_WIDTH
        o2 = o1 + SGU_WIDTH
        o3 = o2 + SGU_WIDTH
        p = z[..., :o1]
        uv = jax.nn.gelu(z[..., o1:o3])
        u, v = uv[..., :SGU_WIDTH], uv[..., SGU_WIDTH:]
        g = jax.nn.sigmoid(z[..., o3:] + b_gates)
        g_a, g_b = g[..., :D], g[..., D:]
        pooled = multiscale_causal_pool(p.reshape(B, S, POOL_GROUPS, POOL_GROUP_DIM))
        y_a = jnp.einsum('bsgc,gcd->bsgd', pooled, w_pool_out).reshape(B, S, D) * pool_scale
        y_b = spatial_gating(u, v, sgu_norm_g, w_spatial, b_spatial) @ w_sgu_out
        x = x + (g_a * y_a + g_b * y_b) @ w_out
        x = x + moe(rmsnorm(x, norm2_g), w_router, b_router, w_gate, b_gate,
                    w_up, b_up, w_down, b_down)
    return rmsnorm(x, normf_g)
```

```python
import functools
import math

import jax
import jax.numpy as jnp
from jax import lax
from jax.experimental import pallas as pl
from jax.experimental.pallas import tpu as pltpu

CHUNK = 64
POOL_WINDOWS = (2, 4, 8, 16)
SGU_HEADS = 8
SGU_BLOCK = 128
TOP_K = 4
SWIGLU_LIMIT = 7.0
SWIGLU_ALPHA = 1.702
EPS = 1e-5

VMEM_LIMIT_BYTES = 56 * 1024 * 1024
POOL_HALO = 16
MOE_TM = 256
TOK_TM = 256

BF16 = jnp.bfloat16
F32 = jnp.float32


def _cparams(*sem):
    return pltpu.CompilerParams(dimension_semantics=sem, vmem_limit_bytes=VMEM_LIMIT_BYTES)


def _gelu_tanh(x):
    c = math.sqrt(2.0 / math.pi)
    return x * (0.5 * (1.0 + jnp.tanh(c * (x + 0.044715 * (x * x * x)))))


def _rms_scale(x):
    return lax.rsqrt(jnp.mean(x * x, axis=-1, keepdims=True) + EPS)


def _rmsnorm_cast_kernel(x_ref, g_ref, o_ref):
    x = x_ref[...]
    o_ref[...] = (x * _rms_scale(x) * g_ref[...]).astype(o_ref.dtype)


def _rmsnorm_cast(x, g, tm=256):
    n, d = x.shape
    return pl.pallas_call(
        _rmsnorm_cast_kernel,
        out_shape=jax.ShapeDtypeStruct((n, d), BF16),
        grid=(n // tm,),
        in_specs=[pl.BlockSpec((tm, d), lambda i: (i, 0)),
                  pl.BlockSpec((1, d), lambda i: (0, 0))],
        out_specs=pl.BlockSpec((tm, d), lambda i: (i, 0)),
        compiler_params=_cparams("parallel"),
        name="rmsnorm_cast",
    )(x, g.reshape(1, d))


def _inproj_plain_kernel(h_ref, w_ref, o_ref, *, act):
    acc = jnp.dot(h_ref[...], w_ref[...], preferred_element_type=F32)
    o_ref[...] = act(acc).astype(o_ref.dtype)


def _inproj_vnorm_kernel(h_ref, w_ref, g_ref, o_ref, *, head_dim):
    acc = _gelu_tanh(jnp.dot(h_ref[...], w_ref[...], preferred_element_type=F32))
    for hh in range(acc.shape[1] // head_dim):
        sl = slice(hh * head_dim, (hh + 1) * head_dim)
        blk = acc[:, sl]
        o_ref[:, sl] = (blk * _rms_scale(blk) * g_ref[:, sl]).astype(o_ref.dtype)


def _inproj_gate_kernel(h_ref, w_ref, b_ref, o_ref):
    acc = jnp.dot(h_ref[...], w_ref[...], preferred_element_type=F32)
    o_ref[...] = jax.nn.sigmoid(acc + b_ref[...]).astype(o_ref.dtype)


def _inproj(body, h, w, col0, ncols, out_dtype, row_vec=None, tm=1024, tn=1024, name="inproj"):
    n, d = h.shape
    off = col0 // tn
    in_specs = [pl.BlockSpec((tm, d), lambda j, i: (i, 0)),
                pl.BlockSpec((d, tn), lambda j, i: (0, j + off))]
    args = [h, w]
    if row_vec is not None:
        in_specs.append(pl.BlockSpec((1, tn), lambda j, i: (0, j)))
        args.append(row_vec.reshape(1, ncols))
    return pl.pallas_call(
        body,
        out_shape=jax.ShapeDtypeStruct((n, ncols), out_dtype),
        grid=(ncols // tn, n // tm),
        in_specs=in_specs,
        out_specs=pl.BlockSpec((tm, tn), lambda j, i: (i, j)),
        compiler_params=_cparams("parallel", "parallel"),
        name=name,
    )(*args)


def _spatial_kernel(u_ref, v_ref, ws_ref, b_ref, s_ref):
    pos_i = lax.broadcasted_iota(jnp.int32, (SGU_BLOCK, SGU_BLOCK), 0)
    pos_j = lax.broadcasted_iota(jnp.int32, (SGU_BLOCK, SGU_BLOCK), 1)
    readable = (pos_j // CHUNK) <= (pos_i // CHUNK)
    ws = jnp.where(readable, ws_ref[0], 0.0).astype(BF16)
    bias = b_ref[0]
    for c in range(u_ref.shape[0] // SGU_BLOCK):
        rows = slice(c * SGU_BLOCK, (c + 1) * SGU_BLOCK)
        mixed = jnp.dot(ws, v_ref[rows, :], preferred_element_type=F32) + bias
        s_ref[rows, :] = (u_ref[rows, :].astype(F32) * mixed).astype(s_ref.dtype)


def _spatial_gate(u, vn, w_spatial, b_spatial, tm=512):
    n, width = u.shape
    hd = width // SGU_HEADS
    return pl.pallas_call(
        _spatial_kernel,
        out_shape=jax.ShapeDtypeStruct((n, width), BF16),
        grid=(n // tm, SGU_HEADS),
        in_specs=[pl.BlockSpec((tm, hd), lambda i, h: (i, h)),
                  pl.BlockSpec((tm, hd), lambda i, h: (i, h)),
                  pl.BlockSpec((1, SGU_BLOCK, SGU_BLOCK), lambda i, h: (h, 0, 0)),
                  pl.BlockSpec((1, SGU_BLOCK, 1), lambda i, h: (h, 0, 0))],
        out_specs=pl.BlockSpec((tm, hd), lambda i, h: (i, h)),
        compiler_params=_cparams("parallel", "parallel"),
        name="spatial_gate",
    )(u, vn, w_spatial, b_spatial.reshape(SGU_HEADS, SGU_BLOCK, 1))


def _mix_kernel(p_ref, halo_ref, s_ref, ga_ref, gb_ref, wp_ref, wsgu_ref, scale_ref, m_ref,
                buf_ref, pooled_ref, *, seq):
    i = pl.program_id(0)
    g = pl.program_id(1)
    tm = p_ref.shape[0]
    seq_pos0 = (i * tm) % seq
    buf_ref[0:POOL_HALO, :] = jnp.where(seq_pos0 == 0, 0.0, halo_ref[...])
    buf_ref[POOL_HALO:POOL_HALO + tm, :] = p_ref[...]
    t1 = seq_pos0 + 1 + lax.broadcasted_iota(jnp.int32, (tm, 1), 0)

    for gi, win in enumerate(POOL_WINDOWS):
        @pl.when(g == gi)
        def _(win=win):
            cur = buf_ref[POOL_HALO:POOL_HALO + tm, :]
            acc = cur
            for k in range(1, win):
                acc = acc + buf_ref[POOL_HALO - k:POOL_HALO - k + tm, :]
            denom = jnp.minimum(t1, win).astype(F32)
            pooled_ref[...] = (acc / denom - cur).astype(pooled_ref.dtype)

    y_a = jnp.dot(pooled_ref[...], wp_ref[0], preferred_element_type=F32) * scale_ref[...]
    y_b = jnp.dot(s_ref[...], wsgu_ref[...], preferred_element_type=F32)
    m = ga_ref[...].astype(F32) * y_a + gb_ref[...].astype(F32) * y_b
    m_ref[...] = m.astype(m_ref.dtype)


def _mix(p, s, gates, w_pool, w_sgu, pool_scale, seq, tm=512):
    n, pw = p.shape
    groups, gd, od = w_pool.shape
    d = groups * od
    sw = s.shape[1]
    assert seq % tm == 0 and tm % POOL_HALO == 0 and max(POOL_WINDOWS) <= POOL_HALO
    hb = tm // POOL_HALO
    return pl.pallas_call(
        functools.partial(_mix_kernel, seq=seq),
        out_shape=jax.ShapeDtypeStruct((n, d), BF16),
        grid=(n // tm, groups),
        in_specs=[pl.BlockSpec((tm, gd), lambda i, g: (i, g)),
                  pl.BlockSpec((POOL_HALO, gd), lambda i, g: (jnp.maximum(i * hb - 1, 0), g)),
                  pl.BlockSpec((tm, sw), lambda i, g: (i, 0)),
                  pl.BlockSpec((tm, od), lambda i, g: (i, g)),
                  pl.BlockSpec((tm, od), lambda i, g: (i, g + groups)),
                  pl.BlockSpec((1, gd, od), lambda i, g: (g, 0, 0)),
                  pl.BlockSpec((sw, od), lambda i, g: (0, g)),
                  pl.BlockSpec((1, od), lambda i, g: (0, g))],
        out_specs=pl.BlockSpec((tm, od), lambda i, g: (i, g)),
        scratch_shapes=[pltpu.VMEM((POOL_HALO + tm, gd), F32),
                        pltpu.VMEM((tm, gd), BF16)],
        compiler_params=_cparams("parallel", "arbitrary"),
        name="mix",
    )(p, p, s, gates, gates, w_pool, w_sgu, pool_scale.reshape(1, d))


def _outproj_kernel(m_ref, w_ref, x_ref, o_ref):
    o_ref[...] = x_ref[...] + jnp.dot(m_ref[...], w_ref[...], preferred_element_type=F32)


def _outproj(m, w, x, tm=512, tn=1024):
    n, d = m.shape
    return pl.pallas_call(
        _outproj_kernel,
        out_shape=jax.ShapeDtypeStruct((n, d), F32),
        grid=(d // tn, n // tm),
        in_specs=[pl.BlockSpec((tm, d), lambda j, i: (i, 0)),
                  pl.BlockSpec((d, tn), lambda j, i: (0, j)),
                  pl.BlockSpec((tm, tn), lambda j, i: (i, j))],
        out_specs=pl.BlockSpec((tm, tn), lambda j, i: (i, j)),
        compiler_params=_cparams("parallel", "parallel"),
        name="outproj",
    )(m, w, x)


def _pack_bf16_pair(lo, hi):
    lo_bits = lax.bitcast_convert_type(lo.astype(BF16).astype(F32), jnp.uint32)
    hi_bits = lax.bitcast_convert_type(hi.astype(BF16).astype(F32), jnp.uint32)
    return (hi_bits & jnp.uint32(0xFFFF0000)) | (lo_bits >> 16)


def _unpack_bf16_pair(words):
    lo = lax.bitcast_convert_type(words << 16, F32).astype(BF16)
    hi = lax.bitcast_convert_type(words & jnp.uint32(0xFFFF0000), F32).astype(BF16)
    return lo, hi


def _router_kernel(x_ref, g_ref, wr_ref, br_ref, h2_ref, topi_ref, rank_ref, topw_ref, cnt_ref,
                   carry_ref):
    i = pl.program_id(0)
    tm, d = x_ref.shape
    n_exp = wr_ref.shape[0]
    half = d // 2

    @pl.when(i == 0)
    def _():
        carry_ref[...] = jnp.zeros_like(carry_ref)

    x = x_ref[...]
    h2 = x * _rms_scale(x) * g_ref[...]
    h2_ref[...] = _pack_bf16_pair(h2[:, :half], h2[:, half:])

    logits = lax.dot_general(wr_ref[...].astype(BF16), h2.astype(BF16),
                             (((1,), (1,)), ((), ())), preferred_element_type=F32)
    logits = logits + br_ref[...]

    e_iota = lax.broadcasted_iota(jnp.int32, (n_exp, tm), 0)
    vals = logits
    top_v, sels = [], []
    for k in range(TOP_K):
        mx = jnp.max(vals, axis=0, keepdims=True)
        idx = jnp.min(jnp.where(vals == mx, e_iota, n_exp), axis=0, keepdims=True)
        sel = e_iota == idx
        vals = jnp.where(sel, -jnp.inf, vals)
        top_v.append(mx)
        sels.append(sel)
        topi_ref[0, k:k + 1, :] = idx

    exps = [jnp.exp(v - top_v[0]) for v in top_v]
    denom = exps[0] + exps[1] + exps[2] + exps[3]
    for k in range(TOP_K):
        topw_ref[k:k + 1, :] = exps[k] / denom

    chosen = jnp.zeros((n_exp, tm), F32)
    for sel in sels:
        chosen = chosen + sel.astype(F32)
    earlier = (lax.broadcasted_iota(jnp.int32, (tm, tm), 0)
               < lax.broadcasted_iota(jnp.int32, (tm, tm), 1)).astype(BF16)
    before = jnp.dot(chosen.astype(BF16), earlier, preferred_element_type=F32) + carry_ref[...]
    for k in range(TOP_K):
        r = jnp.sum(jnp.where(sels[k], before, 0.0), axis=0, keepdims=True)
        rank_ref[0, k:k + 1, :] = r.astype(jnp.int32)
    carry_ref[...] = carry_ref[...] + jnp.sum(chosen, axis=1, keepdims=True)
    cnt_ref[...] = carry_ref[...].astype(jnp.int32)


def _router(x1, g, w_router, b_router, tm=TOK_TM):
    n, d = x1.shape
    n_exp = w_router.shape[1]
    nt = n // tm
    return pl.pallas_call(
        _router_kernel,
        out_shape=(jax.ShapeDtypeStruct((n, d // 2), jnp.uint32),
                   jax.ShapeDtypeStruct((nt, TOP_K, tm), jnp.int32),
                   jax.ShapeDtypeStruct((nt, TOP_K, tm), jnp.int32),
                   jax.ShapeDtypeStruct((TOP_K, n), F32),
                   jax.ShapeDtypeStruct((n_exp, 1), jnp.int32)),
        grid=(nt,),
        in_specs=[pl.BlockSpec((tm, d), lambda i: (i, 0)),
                  pl.BlockSpec((1, d), lambda i: (0, 0)),
                  pl.BlockSpec((n_exp, d), lambda i: (0, 0)),
                  pl.BlockSpec((n_exp, 1), lambda i: (0, 0))],
        out_specs=(pl.BlockSpec((tm, d // 2), lambda i: (i, 0)),
                   pl.BlockSpec((1, TOP_K, tm), lambda i: (i, 0, 0)),
                   pl.BlockSpec((1, TOP_K, tm), lambda i: (i, 0, 0)),
                   pl.BlockSpec((TOP_K, tm), lambda i: (0, i)),
                   pl.BlockSpec((n_exp, 1), lambda i: (0, 0))),
        scratch_shapes=[pltpu.VMEM((n_exp, 1), F32)],
        compiler_params=_cparams("arbitrary"),
        name="router",
    )(x1, g.reshape(1, d), w_router.T, b_router.reshape(n_exp, 1))


def _dispatch_kernel(pstart_ref, zstart_ref, npad_ref, nused_ref, topi_ref, rank_ref, h2_ref, xs_ref,
                     zero_ref, sem):
    i = pl.program_id(0)
    tm = h2_ref.shape[0]
    n_exp = pstart_ref.shape[0]
    n_blocks = xs_ref.shape[0] // MOE_TM

    @pl.when(i == 0)
    def _():
        zero_ref[...] = jnp.zeros_like(zero_ref)

        def zero_copy(row0):
            return pltpu.make_async_copy(
                zero_ref, xs_ref.at[pl.ds(pl.multiple_of(row0, MOE_TM), MOE_TM)], sem)

        def pad_start(e, c):
            @pl.when(npad_ref[e] > 0)
            def _():
                zero_copy(zstart_ref[e]).start()
            return c

        def pad_wait(e, c):
            @pl.when(npad_ref[e] > 0)
            def _():
                zero_copy(zstart_ref[e]).wait()
            return c

        def tail_start(b, c):
            zero_copy(b * MOE_TM).start()
            return c

        def tail_wait(b, c):
            zero_copy(b * MOE_TM).wait()
            return c

        lax.fori_loop(0, n_exp, pad_start, 0)
        lax.fori_loop(nused_ref[0], n_blocks, tail_start, 0)
        lax.fori_loop(0, n_exp, pad_wait, 0)
        lax.fori_loop(nused_ref[0], n_blocks, tail_wait, 0)

    def row_copy(t, k):
        dst = pstart_ref[topi_ref[0, k, t]] + rank_ref[0, k, t]
        return pltpu.make_async_copy(h2_ref.at[pl.ds(t, 1)], xs_ref.at[pl.ds(dst, 1)], sem)

    def start(t, c):
        for k in range(TOP_K):
            row_copy(t, k).start()
        return c

    def wait(t, c):
        for k in range(TOP_K):
            row_copy(t, k).wait()
        return c

    lax.fori_loop(0, tm, start, 0)
    lax.fori_loop(0, tm, wait, 0)


def _dispatch(pstart, zstart, npad, nused, topi, rank, h2, n_rows, tm=TOK_TM):
    n, hw = h2.shape
    return pl.pallas_call(
        _dispatch_kernel,
        out_shape=jax.ShapeDtypeStruct((n_rows, hw), jnp.uint32),
        grid_spec=pltpu.PrefetchScalarGridSpec(
            num_scalar_prefetch=4,
            grid=(n // tm,),
            in_specs=[pl.BlockSpec((1, TOP_K, tm), lambda i, *_: (i, 0, 0), memory_space=pltpu.SMEM),
                      pl.BlockSpec((1, TOP_K, tm), lambda i, *_: (i, 0, 0), memory_space=pltpu.SMEM),
                      pl.BlockSpec((tm, hw), lambda i, *_: (i, 0))],
            out_specs=pl.BlockSpec(memory_space=pl.ANY),
            scratch_shapes=[pltpu.VMEM((MOE_TM, hw), jnp.uint32),
                            pltpu.SemaphoreType.DMA(())],
        ),
        compiler_params=_cparams("arbitrary"),
        name="dispatch",
    )(pstart, zstart, npad, nused, topi, rank, h2)


def _expert_up_kernel(bidx_ref, be_ref, nused_ref, xs_ref, wg_ref, wu_ref, bg_ref, bu_ref, hm_ref):
    b = pl.program_id(0)

    @pl.when(b < nused_ref[0])
    def _():
        half = xs_ref.shape[1]
        lo, hi = _unpack_bf16_pair(xs_ref[...])

        def proj(w_ref, b_ref):
            return (jnp.dot(lo, w_ref[0, :half, :], preferred_element_type=F32)
                    + jnp.dot(hi, w_ref[0, half:, :], preferred_element_type=F32) + b_ref[0])

        gate = jnp.minimum(proj(wg_ref, bg_ref), SWIGLU_LIMIT)
        up = jnp.clip(proj(wu_ref, bu_ref), -SWIGLU_LIMIT, SWIGLU_LIMIT)
        glu = gate * jax.nn.sigmoid(SWIGLU_ALPHA * gate)
        hm_ref[...] = ((up + 1.0) * glu).astype(hm_ref.dtype)

    @pl.when(b >= nused_ref[0])
    def _():
        hm_ref[...] = jnp.zeros_like(hm_ref)


def _expert_up(bidx, be, nused, xs, wg, wu, bg, bu):
    n_rows, hw = xs.shape
    n_exp, d, f = wg.shape
    return pl.pallas_call(
        _expert_up_kernel,
        out_shape=jax.ShapeDtypeStruct((n_rows, f), BF16),
        grid_spec=pltpu.PrefetchScalarGridSpec(
            num_scalar_prefetch=3,
            grid=(n_rows // MOE_TM,),
            in_specs=[pl.BlockSpec((MOE_TM, hw), lambda b, bi, be, nu: (bi[b], 0)),
                      pl.BlockSpec((1, d, f), lambda b, bi, be, nu: (be[b], 0, 0)),
                      pl.BlockSpec((1, d, f), lambda b, bi, be, nu: (be[b], 0, 0)),
                      pl.BlockSpec((1, 1, f), lambda b, bi, be, nu: (be[b], 0, 0)),
                      pl.BlockSpec((1, 1, f), lambda b, bi, be, nu: (be[b], 0, 0))],
            out_specs=pl.BlockSpec((MOE_TM, f), lambda b, bi, be, nu: (b, 0)),
        ),
        compiler_params=_cparams("arbitrary"),
        name="expert_up",
    )(bidx, be, nused, xs, wg, wu, bg.reshape(n_exp, 1, f), bu.reshape(n_exp, 1, f))


def _expert_down_kernel(bidx_ref, be_ref, nused_ref, hm_ref, wd_ref, bd_ref, y_ref):
    b = pl.program_id(0)

    @pl.when(b < nused_ref[0])
    def _():
        y_ref[...] = jnp.dot(hm_ref[...], wd_ref[0], preferred_element_type=F32) + bd_ref[0]

    @pl.when(b >= nused_ref[0])
    def _():
        y_ref[...] = jnp.zeros_like(y_ref)


def _expert_down(bidx, be, nused, hm, wd, bd):
    n_rows, f = hm.shape
    n_exp, _, d = wd.shape
    return pl.pallas_call(
        _expert_down_kernel,
        out_shape=jax.ShapeDtypeStruct((n_rows, d), F32),
        grid_spec=pltpu.PrefetchScalarGridSpec(
            num_scalar_prefetch=3,
            grid=(n_rows // MOE_TM,),
            in_specs=[pl.BlockSpec((MOE_TM, f), lambda b, bi, be, nu: (bi[b], 0)),
                      pl.BlockSpec((1, f, d), lambda b, bi, be, nu: (be[b], 0, 0)),
                      pl.BlockSpec((1, 1, d), lambda b, bi, be, nu: (be[b], 0, 0))],
            out_specs=pl.BlockSpec((MOE_TM, d), lambda b, bi, be, nu: (b, 0)),
        ),
        compiler_params=_cparams("arbitrary"),
        name="expert_down",
    )(bidx, be, nused, hm, wd, bd.reshape(n_exp, 1, d))


def _combine_kernel(pstart_ref, topi_ref, rank_ref, x1_ref, w_ref, g_ref, y_ref, o_ref, buf_ref, sem):
    tm = x1_ref.shape[0]

    def row_copy(t, k):
        src = pstart_ref[topi_ref[0, k, t]] + rank_ref[0, k, t]
        return pltpu.make_async_copy(y_ref.at[pl.ds(src, 1)], buf_ref.at[k, pl.ds(t, 1)], sem)

    def start(t, c):
        for k in range(TOP_K):
            row_copy(t, k).start()
        return c

    def wait(t, c):
        for k in range(TOP_K):
            row_copy(t, k).wait()
        return c

    lax.fori_loop(0, tm, start, 0)
    lax.fori_loop(0, tm, wait, 0)

    acc = x1_ref[...]
    for k in range(TOP_K):
        acc = acc + w_ref[:, k:k + 1] * buf_ref[k]
    o_ref[...] = acc * _rms_scale(acc) * g_ref[...]


def _combine(pstart, topi, rank, x1, topw_cols, g, y, tm=TOK_TM):
    n, d = x1.shape
    return pl.pallas_call(
        _combine_kernel,
        out_shape=jax.ShapeDtypeStruct((n, d), F32),
        grid_spec=pltpu.PrefetchScalarGridSpec(
            num_scalar_prefetch=1,
            grid=(n // tm,),
            in_specs=[pl.BlockSpec((1, TOP_K, tm), lambda i, *_: (i, 0, 0), memory_space=pltpu.SMEM),
                      pl.BlockSpec((1, TOP_K, tm), lambda i, *_: (i, 0, 0), memory_space=pltpu.SMEM),
                      pl.BlockSpec((tm, d), lambda i, *_: (i, 0)),
                      pl.BlockSpec((tm, TOP_K), lambda i, *_: (i, 0)),
                      pl.BlockSpec((1, d), lambda i, *_: (0, 0)),
                      pl.BlockSpec(memory_space=pl.ANY)],
            out_specs=pl.BlockSpec((tm, d), lambda i, *_: (i, 0)),
            scratch_shapes=[pltpu.VMEM((TOP_K, tm, d), F32),
                            pltpu.SemaphoreType.DMA(())],
        ),
        compiler_params=_cparams("arbitrary"),
        name="combine",
    )(pstart, topi, rank, x1, topw_cols, g.reshape(1, d), y)


def kernel(x, norm1_g, w_in, b_gates, w_pool_out, pool_scale, sgu_norm_g, w_spatial, b_spatial,
           w_sgu_out, w_out, norm2_g, w_router, b_router, w_gate, b_gate, w_up, b_up, w_down,
           b_down, normf_g):
    bsz, seq, d = x.shape
    n = bsz * seq
    pool_w = w_pool_out.shape[0] * w_pool_out.shape[1]
    sgu_w = w_sgu_out.shape[0]
    n_exp = w_router.shape[1]
    xf = x.reshape(n, d)

    w_in_b = w_in.astype(BF16)
    h = _rmsnorm_cast(xf, norm1_g)
    p = _inproj(functools.partial(_inproj_plain_kernel, act=lambda a: a), h, w_in_b,
                0, pool_w, F32, name="inproj_p")
    u = _inproj(functools.partial(_inproj_plain_kernel, act=_gelu_tanh), h, w_in_b,
                pool_w, sgu_w, BF16, name="inproj_u")
    vn = _inproj(functools.partial(_inproj_vnorm_kernel, head_dim=sgu_w // SGU_HEADS), h, w_in_b,
                 pool_w + sgu_w, sgu_w, BF16, row_vec=sgu_norm_g, name="inproj_v")
    gates = _inproj(_inproj_gate_kernel, h, w_in_b, pool_w + 2 * sgu_w, 2 * d, BF16,
                    row_vec=b_gates, name="inproj_gates")

    s = _spatial_gate(u, vn, w_spatial, b_spatial)
    m = _mix(p, s, gates, w_pool_out.astype(BF16), w_sgu_out.astype(BF16), pool_scale, seq)
    x1 = _outproj(m, w_out.astype(BF16), xf)

    h2, topi, rank, topw, counts = _router(x1, norm2_g, w_router, b_router)

    counts = counts[:, 0]
    padded = ((counts + MOE_TM - 1) // MOE_TM) * MOE_TM
    pend = jnp.cumsum(padded)
    pstart = (pend - padded).astype(jnp.int32)
    n_blocks = (n * TOP_K) // MOE_TM + n_exp
    n_rows = n_blocks * MOE_TM
    nused = (pend[-1] // MOE_TM).astype(jnp.int32)
    blk = jnp.minimum(jnp.arange(n_blocks, dtype=jnp.int32), nused - 1)
    blk_e = jnp.minimum(jnp.searchsorted(pend, blk * MOE_TM, side="right"), n_exp - 1).astype(jnp.int32)
    zstart = jnp.maximum(pend - MOE_TM, 0).astype(jnp.int32)
    npad = (padded - counts).astype(jnp.int32)

    nused1 = nused.reshape(1)
    xs = _dispatch(pstart, zstart, npad, nused1, topi, rank, h2, n_rows)
    hm = _expert_up(blk, blk_e, nused1, xs, w_gate.astype(BF16), w_up.astype(BF16), b_gate, b_up)
    y = _expert_down(blk, blk_e, nused1, hm, w_down.astype(BF16), b_down)
    out = _combine(pstart, topi, rank, x1, topw.T, normf_g, y)
    return out.reshape(bsz, seq, d)
```

```python
import functools
import math

import jax
import jax.numpy as jnp
from jax import lax
from jax.experimental import pallas as pl
from jax.experimental.pallas import tpu as pltpu

CHUNK = 64
POOL_WINDOWS = (2, 4, 8, 16)
SGU_HEADS = 8
SGU_BLOCK = 128
TOP_K = 4
SWIGLU_LIMIT = 7.0
SWIGLU_ALPHA = 1.702
EPS = 1e-5

VMEM_LIMIT_BYTES = 56 * 1024 * 1024
POOL_HALO = 16
MOE_TM = 256
MOE_TF = 512
TOK_TM = 256
COMB_TM = 128
DMA_UNROLL = 8

BF16 = jnp.bfloat16
F32 = jnp.float32
I32 = jnp.int32


def _cparams(*sem):
    return pltpu.CompilerParams(dimension_semantics=sem, vmem_limit_bytes=VMEM_LIMIT_BYTES)


def _gelu_tanh(x):
    c = math.sqrt(2.0 / math.pi)
    return x * (0.5 * (1.0 + jnp.tanh(c * (x + 0.044715 * (x * x * x)))))


def _rms_scale(x):
    return lax.rsqrt(jnp.mean(x * x, axis=-1, keepdims=True) + EPS)


def _rmsnorm_cast_kernel(x_ref, g_ref, o_ref):
    x = x_ref[...]
    o_ref[...] = (x * _rms_scale(x) * g_ref[...]).astype(o_ref.dtype)


def _rmsnorm_cast(x, g, tm=256):
    n, d = x.shape
    return pl.pallas_call(
        _rmsnorm_cast_kernel,
        out_shape=jax.ShapeDtypeStruct((n, d), BF16),
        grid=(n // tm,),
        in_specs=[pl.BlockSpec((tm, d), lambda i: (i, 0)),
                  pl.BlockSpec((1, d), lambda i: (0, 0))],
        out_specs=pl.BlockSpec((tm, d), lambda i: (i, 0)),
        compiler_params=_cparams("parallel"),
        name="rmsnorm_cast",
    )(x, g.reshape(1, d))


def _inproj_pool_kernel(h_ref, w_ref, o_ref, buf_ref, carry_ref, *, seq):
    j = pl.program_id(0)
    i = pl.program_id(1)
    tm = h_ref.shape[0]
    p = jnp.dot(h_ref[...], w_ref[...], preferred_element_type=F32)
    seq_pos0 = (i * tm) % seq
    buf_ref[0:POOL_HALO, :] = jnp.where(seq_pos0 == 0, 0.0, carry_ref[...])
    buf_ref[POOL_HALO:, :] = p
    carry_ref[...] = p[tm - POOL_HALO:, :]
    t1 = seq_pos0 + 1 + lax.broadcasted_iota(I32, (tm, 1), 0)

    for gi, win in enumerate(POOL_WINDOWS):
        @pl.when(j == gi)
        def _(win=win):
            s = buf_ref[...]
            shift = 1
            while shift < win:
                s = s + pltpu.roll(s, shift, 0)
                shift *= 2
            denom = jnp.minimum(t1, win).astype(F32)
            o_ref[...] = (s[POOL_HALO:, :] / denom - p).astype(o_ref.dtype)


def _inproj_pool(h, w, n_groups, gd, seq, tm=1024):
    n, d = h.shape
    assert seq % tm == 0 and n_groups == len(POOL_WINDOWS) and max(POOL_WINDOWS) <= POOL_HALO
    return pl.pallas_call(
        functools.partial(_inproj_pool_kernel, seq=seq),
        out_shape=jax.ShapeDtypeStruct((n, n_groups * gd), BF16),
        grid=(n_groups, n // tm),
        in_specs=[pl.BlockSpec((tm, d), lambda j, i: (i, 0)),
                  pl.BlockSpec((d, gd), lambda j, i: (0, j))],
        out_specs=pl.BlockSpec((tm, gd), lambda j, i: (i, j)),
        scratch_shapes=[pltpu.VMEM((POOL_HALO + tm, gd), F32),
                        pltpu.VMEM((POOL_HALO, gd), F32)],
        compiler_params=_cparams("arbitrary", "arbitrary"),
        name="inproj_pool",
    )(h, w)


def _inproj_sgu_kernel(h_ref, wu_ref, wv_ref, g_ref, ws_ref, b_ref, s_ref, *, head_dim):
    h = h_ref[...]
    u = _gelu_tanh(jnp.dot(h, wu_ref[...], preferred_element_type=F32))
    v = _gelu_tanh(jnp.dot(h, wv_ref[...], preferred_element_type=F32))
    pos_i = lax.broadcasted_iota(I32, (SGU_BLOCK, SGU_BLOCK), 0)
    pos_j = lax.broadcasted_iota(I32, (SGU_BLOCK, SGU_BLOCK), 1)
    readable = (pos_j // CHUNK) <= (pos_i // CHUNK)
    for hh in range(u.shape[1] // head_dim):
        cols = slice(hh * head_dim, (hh + 1) * head_dim)
        blk = v[:, cols]
        vn = (blk * _rms_scale(blk) * g_ref[:, cols]).astype(BF16)
        ws = jnp.where(readable, ws_ref[hh], 0.0).astype(BF16)
        bias = b_ref[hh]
        for c in range(u.shape[0] // SGU_BLOCK):
            rows = slice(c * SGU_BLOCK, (c + 1) * SGU_BLOCK)
            mixed = jnp.dot(ws, vn[rows, :], preferred_element_type=F32) + bias
            s_ref[rows, cols] = (u[rows, cols] * mixed).astype(s_ref.dtype)


def _inproj_sgu(h, w, u_col0, v_col0, width, norm_g, w_spatial, b_spatial, tm=1024, tn=512):
    n, d = h.shape
    hd = width // SGU_HEADS
    hpt = tn // hd
    return pl.pallas_call(
        functools.partial(_inproj_sgu_kernel, head_dim=hd),
        out_shape=jax.ShapeDtypeStruct((n, width), BF16),
        grid=(width // tn, n // tm),
        in_specs=[pl.BlockSpec((tm, d), lambda j, i: (i, 0)),
                  pl.BlockSpec((d, tn), lambda j, i: (0, j + u_col0 // tn)),
                  pl.BlockSpec((d, tn), lambda j, i: (0, j + v_col0 // tn)),
                  pl.BlockSpec((1, tn), lambda j, i: (0, j)),
                  pl.BlockSpec((hpt, SGU_BLOCK, SGU_BLOCK), lambda j, i: (j, 0, 0)),
                  pl.BlockSpec((hpt, SGU_BLOCK, 1), lambda j, i: (j, 0, 0))],
        out_specs=pl.BlockSpec((tm, tn), lambda j, i: (i, j)),
        compiler_params=_cparams("parallel", "parallel"),
        name="inproj_sgu",
    )(h, w, w, norm_g.reshape(1, width), w_spatial, b_spatial.reshape(SGU_HEADS, SGU_BLOCK, 1))


def _mix_kernel(h_ref, wga_ref, wgb_ref, ba_ref, bb_ref, pooled_ref, wp_ref, scale_ref,
                s_ref, wsgu_ref, m_ref):
    h = h_ref[...]
    g_a = jax.nn.sigmoid(jnp.dot(h, wga_ref[...], preferred_element_type=F32) + ba_ref[...])
    y_a = jnp.dot(pooled_ref[...], wp_ref[0], preferred_element_type=F32) * scale_ref[...]
    m = g_a * y_a
    g_b = jax.nn.sigmoid(jnp.dot(h, wgb_ref[...], preferred_element_type=F32) + bb_ref[...])
    y_b = jnp.dot(s_ref[...], wsgu_ref[...], preferred_element_type=F32)
    m_ref[...] = (m + g_b * y_b).astype(m_ref.dtype)


def _mix(h, w_in, gate_col0, b_gates, pooled, w_pool, pool_scale, s, w_sgu, tm=512, tn=512):
    n, d = h.shape
    groups, gd, od = w_pool.shape
    sw = s.shape[1]
    tpg = od // tn
    a0 = gate_col0 // tn
    nd = d // tn
    return pl.pallas_call(
        _mix_kernel,
        out_shape=jax.ShapeDtypeStruct((n, d), BF16),
        grid=(nd, n // tm),
        in_specs=[pl.BlockSpec((tm, d), lambda j, i: (i, 0)),
                  pl.BlockSpec((d, tn), lambda j, i: (0, j + a0)),
                  pl.BlockSpec((d, tn), lambda j, i: (0, j + a0 + nd)),
                  pl.BlockSpec((1, tn), lambda j, i: (0, j)),
                  pl.BlockSpec((1, tn), lambda j, i: (0, j + nd)),
                  pl.BlockSpec((tm, gd), lambda j, i: (i, j // tpg)),
                  pl.BlockSpec((1, gd, tn), lambda j, i: (j // tpg, 0, j % tpg)),
                  pl.BlockSpec((1, tn), lambda j, i: (0, j)),
                  pl.BlockSpec((tm, sw), lambda j, i: (i, 0)),
                  pl.BlockSpec((sw, tn), lambda j, i: (0, j))],
        out_specs=pl.BlockSpec((tm, tn), lambda j, i: (i, j)),
        compiler_params=_cparams("parallel", "parallel"),
        name="mix",
    )(h, w_in, w_in, b_gates.reshape(1, 2 * d), b_gates.reshape(1, 2 * d), pooled, w_pool,
      pool_scale.reshape(1, d), s, w_sgu)


def _outproj_kernel(m_ref, w_ref, x_ref, o_ref):
    o_ref[...] = x_ref[...] + jnp.dot(m_ref[...], w_ref[...], preferred_element_type=F32)


def _outproj(m, w, x, tm=512, tn=1024):
    n, d = m.shape
    return pl.pallas_call(
        _outproj_kernel,
        out_shape=jax.ShapeDtypeStruct((n, d), F32),
        grid=(d // tn, n // tm),
        in_specs=[pl.BlockSpec((tm, d), lambda j, i: (i, 0)),
                  pl.BlockSpec((d, tn), lambda j, i: (0, j)),
                  pl.BlockSpec((tm, tn), lambda j, i: (i, j))],
        out_specs=pl.BlockSpec((tm, tn), lambda j, i: (i, j)),
        compiler_params=_cparams("parallel", "parallel"),
        name="outproj",
    )(m, w, x)


def _pack_bf16_pair(lo, hi):
    lo_bits = lax.bitcast_convert_type(lo.astype(BF16).astype(F32), jnp.uint32)
    hi_bits = lax.bitcast_convert_type(hi.astype(BF16).astype(F32), jnp.uint32)
    return (hi_bits & jnp.uint32(0xFFFF0000)) | (lo_bits >> 16)


def _unpack_bf16_pair(words):
    lo = lax.bitcast_convert_type(words << 16, F32).astype(BF16)
    hi = lax.bitcast_convert_type(words & jnp.uint32(0xFFFF0000), F32).astype(BF16)
    return lo, hi


def _router_kernel(x_ref, g_ref, wr_ref, br_ref, h2_ref, topi_ref, rank_ref, topw_ref, cnt_ref,
                   carry_ref):
    i = pl.program_id(0)
    tm, d = x_ref.shape
    n_exp = wr_ref.shape[0]
    half = d // 2

    @pl.when(i == 0)
    def _():
        carry_ref[...] = jnp.zeros_like(carry_ref)

    x = x_ref[...]
    h2 = x * _rms_scale(x) * g_ref[...]
    h2_ref[...] = _pack_bf16_pair(h2[:, :half], h2[:, half:])

    logits = lax.dot_general(wr_ref[...].astype(BF16), h2.astype(BF16),
                             (((1,), (1,)), ((), ())), preferred_element_type=F32)
    logits = logits + br_ref[...]

    e_iota = lax.broadcasted_iota(I32, (n_exp, tm), 0)
    vals = logits
    top_v, sels = [], []
    for k in range(TOP_K):
        mx = jnp.max(vals, axis=0, keepdims=True)
        idx = jnp.min(jnp.where(vals == mx, e_iota, n_exp), axis=0, keepdims=True)
        sel = e_iota == idx
        vals = jnp.where(sel, -jnp.inf, vals)
        top_v.append(mx)
        sels.append(sel)
        topi_ref[0, k:k + 1, :] = idx

    exps = [jnp.exp(v - top_v[0]) for v in top_v]
    denom = exps[0] + exps[1] + exps[2] + exps[3]
    for k in range(TOP_K):
        topw_ref[k:k + 1, :] = exps[k] / denom

    chosen = jnp.zeros((n_exp, tm), F32)
    for sel in sels:
        chosen = chosen + sel.astype(F32)
    earlier = (lax.broadcasted_iota(I32, (tm, tm), 0)
               < lax.broadcasted_iota(I32, (tm, tm), 1)).astype(BF16)
    before = jnp.dot(chosen.astype(BF16), earlier, preferred_element_type=F32) + carry_ref[...]
    for k in range(TOP_K):
        r = jnp.sum(jnp.where(sels[k], before, 0.0), axis=0, keepdims=True)
        rank_ref[0, k:k + 1, :] = r.astype(I32)
    carry_ref[...] = carry_ref[...] + jnp.sum(chosen, axis=1, keepdims=True)
    cnt_ref[...] = carry_ref[...].astype(I32)


def _router(x1, g, w_router, b_router, tm=TOK_TM):
    n, d = x1.shape
    n_exp = w_router.shape[1]
    nt = n // tm
    return pl.pallas_call(
        _router_kernel,
        out_shape=(jax.ShapeDtypeStruct((n, d // 2), jnp.uint32),
                   jax.ShapeDtypeStruct((nt, TOP_K, tm), I32),
                   jax.ShapeDtypeStruct((nt, TOP_K, tm), I32),
                   jax.ShapeDtypeStruct((TOP_K, n), F32),
                   jax.ShapeDtypeStruct((n_exp, 1), I32)),
        grid=(nt,),
        in_specs=[pl.BlockSpec((tm, d), lambda i: (i, 0)),
                  pl.BlockSpec((1, d), lambda i: (0, 0)),
                  pl.BlockSpec((n_exp, d), lambda i: (0, 0)),
                  pl.BlockSpec((n_exp, 1), lambda i: (0, 0))],
        out_specs=(pl.BlockSpec((tm, d // 2), lambda i: (i, 0)),
                   pl.BlockSpec((1, TOP_K, tm), lambda i: (i, 0, 0)),
                   pl.BlockSpec((1, TOP_K, tm), lambda i: (i, 0, 0)),
                   pl.BlockSpec((TOP_K, tm), lambda i: (0, i)),
                   pl.BlockSpec((n_exp, 1), lambda i: (0, 0))),
        scratch_shapes=[pltpu.VMEM((n_exp, 1), F32)],
        compiler_params=_cparams("arbitrary"),
        name="router",
    )(x1, g.reshape(1, d), w_router.T, b_router.reshape(n_exp, 1))


def _dispatch_kernel(zstart_ref, npad_ref, nused_ref, dest_ref, h2_ref, xs_ref, zero_ref, sem):
    i = pl.program_id(0)
    tm = h2_ref.shape[0]
    n_exp = zstart_ref.shape[0]
    n_blocks = xs_ref.shape[0] // MOE_TM

    @pl.when(i == 0)
    def _():
        zero_ref[...] = jnp.zeros_like(zero_ref)

        def zero_copy(row0):
            return pltpu.make_async_copy(
                zero_ref, xs_ref.at[pl.ds(pl.multiple_of(row0, MOE_TM), MOE_TM)], sem)

        def pad_start(e, c):
            @pl.when(npad_ref[e] > 0)
            def _():
                zero_copy(zstart_ref[e]).start()
            return c

        def pad_wait(e, c):
            @pl.when(npad_ref[e] > 0)
            def _():
                zero_copy(zstart_ref[e]).wait()
            return c

        def tail_start(b, c):
            zero_copy(b * MOE_TM).start()
            return c

        def tail_wait(b, c):
            zero_copy(b * MOE_TM).wait()
            return c

        lax.fori_loop(0, n_exp, pad_start, 0)
        lax.fori_loop(nused_ref[0], n_blocks, tail_start, 0)
        lax.fori_loop(0, n_exp, pad_wait, 0)
        lax.fori_loop(nused_ref[0], n_blocks, tail_wait, 0)

    def start(t, c):
        for k in range(TOP_K):
            pltpu.make_async_copy(h2_ref.at[pl.ds(t, 1)],
                                  xs_ref.at[pl.ds(dest_ref[0, k, t], 1)], sem).start()
        return c

    def wait(t, c):
        for k in range(TOP_K):
            pltpu.make_async_copy(h2_ref.at[pl.ds(0, 1)], xs_ref.at[pl.ds(0, 1)], sem).wait()
        return c

    lax.fori_loop(0, tm, start, 0, unroll=DMA_UNROLL)
    lax.fori_loop(0, tm, wait, 0, unroll=DMA_UNROLL)


def _dispatch(zstart, npad, nused, dest, h2, n_rows, tm=TOK_TM):
    n, hw = h2.shape
    return pl.pallas_call(
        _dispatch_kernel,
        out_shape=jax.ShapeDtypeStruct((n_rows, hw), jnp.uint32),
        grid_spec=pltpu.PrefetchScalarGridSpec(
            num_scalar_prefetch=3,
            grid=(n // tm,),
            in_specs=[pl.BlockSpec((1, TOP_K, tm), lambda i, *_: (i, 0, 0), memory_space=pltpu.SMEM),
                      pl.BlockSpec((tm, hw), lambda i, *_: (i, 0))],
            out_specs=pl.BlockSpec(memory_space=pl.ANY),
            scratch_shapes=[pltpu.VMEM((MOE_TM, hw), jnp.uint32),
                            pltpu.SemaphoreType.DMA(())],
        ),
        compiler_params=_cparams("arbitrary"),
        name="dispatch",
    )(zstart, npad, nused, dest, h2)


def _expert_up_kernel(in_rb, in_e, in_f, out_rb, out_f, flags, xs_ref, wg_ref, wu_ref, bg_ref, bu_ref,
                      hm_ref, wg_s, wu_s):
    fl = flags[pl.program_id(0)]

    @pl.when((fl & 2) != 0)
    def _():
        wg_s[...] = wg_ref[0].astype(BF16)
        wu_s[...] = wu_ref[0].astype(BF16)

    @pl.when((fl & 1) != 0)
    def _():
        half = xs_ref.shape[1]
        lo, hi = _unpack_bf16_pair(xs_ref[...])

        def proj(w_s, b_ref):
            return (jnp.dot(lo, w_s[:half, :], preferred_element_type=F32)
                    + jnp.dot(hi, w_s[half:, :], preferred_element_type=F32) + b_ref[0])

        gate = jnp.minimum(proj(wg_s, bg_ref), SWIGLU_LIMIT)
        up = jnp.clip(proj(wu_s, bu_ref), -SWIGLU_LIMIT, SWIGLU_LIMIT)
        glu = gate * jax.nn.sigmoid(SWIGLU_ALPHA * gate)
        hm_ref[...] = ((up + 1.0) * glu).astype(hm_ref.dtype)

    @pl.when((fl & 1) == 0)
    def _():
        hm_ref[...] = jnp.zeros_like(hm_ref)


def _expert_up(items, xs, wg, wu, bg, bu):
    n_rows, hw = xs.shape
    n_exp, d, f = wg.shape
    n_items = items[0].shape[0]
    w_spec = pl.BlockSpec((1, d, MOE_TF), lambda q, rb, e, ft, orb, oft, fl: (e[q], 0, ft[q]))
    b_spec = pl.BlockSpec((1, 1, MOE_TF), lambda q, rb, e, ft, orb, oft, fl: (e[q], 0, ft[q]))
    return pl.pallas_call(
        _expert_up_kernel,
        out_shape=jax.ShapeDtypeStruct((n_rows, f), BF16),
        grid_spec=pltpu.PrefetchScalarGridSpec(
            num_scalar_prefetch=6,
            grid=(n_items,),
            in_specs=[pl.BlockSpec((MOE_TM, hw), lambda q, rb, e, ft, orb, oft, fl: (rb[q], 0)),
                      w_spec, w_spec, b_spec, b_spec],
            out_specs=pl.BlockSpec((MOE_TM, MOE_TF),
                                   lambda q, rb, e, ft, orb, oft, fl: (orb[q], oft[q])),
            scratch_shapes=[pltpu.VMEM((d, MOE_TF), BF16), pltpu.VMEM((d, MOE_TF), BF16)],
        ),
        compiler_params=_cparams("arbitrary"),
        name="expert_up",
    )(*items, xs, wg, wu, bg.reshape(n_exp, 1, f), bu.reshape(n_exp, 1, f))


def _expert_down_kernel(bidx_ref, be_ref, flags, hm_ref, wd_ref, bd_ref, y_ref, wd_s):
    fl = flags[pl.program_id(0)]

    @pl.when((fl & 2) != 0)
    def _():
        wd_s[...] = wd_ref[0].astype(BF16)

    @pl.when((fl & 1) != 0)
    def _():
        y_ref[...] = jnp.dot(hm_ref[...], wd_s[...], preferred_element_type=F32) + bd_ref[0]

    @pl.when((fl & 1) == 0)
    def _():
        y_ref[...] = jnp.zeros_like(y_ref)


def _expert_down(bidx, be, flags, hm, wd, bd):
    n_rows, f = hm.shape
    n_exp, _, d = wd.shape
    return pl.pallas_call(
        _expert_down_kernel,
        out_shape=jax.ShapeDtypeStruct((n_rows, d), F32),
        grid_spec=pltpu.PrefetchScalarGridSpec(
            num_scalar_prefetch=3,
            grid=(n_rows // MOE_TM,),
            in_specs=[pl.BlockSpec((MOE_TM, f), lambda b, bi, be, fl: (bi[b], 0)),
                      pl.BlockSpec((1, f, d), lambda b, bi, be, fl: (be[b], 0, 0)),
                      pl.BlockSpec((1, 1, d), lambda b, bi, be, fl: (be[b], 0, 0))],
            out_specs=pl.BlockSpec((MOE_TM, d), lambda b, bi, be, fl: (b, 0)),
            scratch_shapes=[pltpu.VMEM((f, d), BF16)],
        ),
        compiler_params=_cparams("arbitrary"),
        name="expert_down",
    )(bidx, be, flags, hm, wd, bd.reshape(n_exp, 1, d))


def _combine_kernel(dest_ref, dest_next_ref, x1_ref, w_ref, g_ref, y_ref, o_ref, buf_ref, sems):
    i = pl.program_id(0)
    nt = pl.num_programs(0)
    tm = x1_ref.shape[0]
    slot = i % 2

    def issue(d_ref, sl):
        def start(t, c):
            for k in range(TOP_K):
                pltpu.make_async_copy(y_ref.at[pl.ds(d_ref[0, k, t], 1)],
                                      buf_ref.at[sl, k, pl.ds(t, 1)], sems.at[sl]).start()
            return c
        lax.fori_loop(0, tm, start, 0, unroll=DMA_UNROLL)

    @pl.when(i == 0)
    def _():
        issue(dest_ref, 0)

    @pl.when(i + 1 < nt)
    def _():
        issue(dest_next_ref, 1 - slot)

    def wait(t, c):
        for k in range(TOP_K):
            pltpu.make_async_copy(y_ref.at[pl.ds(0, 1)], buf_ref.at[slot, 0, pl.ds(0, 1)],
                                  sems.at[slot]).wait()
        return c

    lax.fori_loop(0, tm, wait, 0, unroll=DMA_UNROLL)

    acc = x1_ref[...]
    for k in range(TOP_K):
        acc = acc + w_ref[:, k:k + 1] * buf_ref[slot, k]
    o_ref[...] = acc * _rms_scale(acc) * g_ref[...]


def _combine(dest, x1, topw_cols, g, y, tm=COMB_TM):
    n, d = x1.shape
    nt = n // tm
    return pl.pallas_call(
        _combine_kernel,
        out_shape=jax.ShapeDtypeStruct((n, d), F32),
        grid=(nt,),
        in_specs=[pl.BlockSpec((1, TOP_K, tm), lambda i: (i, 0, 0), memory_space=pltpu.SMEM),
                  pl.BlockSpec((1, TOP_K, tm), lambda i: (jnp.minimum(i + 1, nt - 1), 0, 0),
                               memory_space=pltpu.SMEM),
                  pl.BlockSpec((tm, d), lambda i: (i, 0)),
                  pl.BlockSpec((tm, TOP_K), lambda i: (i, 0)),
                  pl.BlockSpec((1, d), lambda i: (0, 0)),
                  pl.BlockSpec(memory_space=pl.ANY)],
        out_specs=pl.BlockSpec((tm, d), lambda i: (i, 0)),
        scratch_shapes=[pltpu.VMEM((2, TOP_K, tm, d), F32),
                        pltpu.SemaphoreType.DMA((2,))],
        compiler_params=_cparams("arbitrary"),
        name="combine",
    )(dest, dest, x1, topw_cols, g.reshape(1, d), y)


def _routing_tables(counts, topi, rank, n_blocks, n_ftiles):
    n_exp = counts.shape[0]
    nb = (counts + MOE_TM - 1) // MOE_TM
    bend = jnp.cumsum(nb)
    bstart = bend - nb
    nused = bend[-1]
    pstart = bstart * MOE_TM
    onehot = topi[..., None] == jnp.arange(n_exp, dtype=I32)
    dest = jnp.sum(jnp.where(onehot, pstart, 0), axis=-1).astype(I32) + rank
    b = jnp.arange(n_blocks, dtype=I32)
    bc = jnp.minimum(b, nused - 1)
    be = jnp.sum(bend[None, :] <= bc[:, None], axis=1).astype(I32)
    bvalid = b < nused
    bflags = (bvalid.astype(I32) + 2 * (bvalid & (bc == bstart[be])).astype(I32))
    q = jnp.arange(n_ftiles * n_blocks, dtype=I32)
    total = n_ftiles * nused
    qc = jnp.minimum(q, total - 1)
    qe = jnp.sum((n_ftiles * bend)[None, :] <= qc[:, None], axis=1).astype(I32)
    local = qc - n_ftiles * bstart[qe]
    nbe = jnp.maximum(nb[qe], 1)
    in_f = local // nbe
    in_rb = bstart[qe] + local % nbe
    qvalid = q < total
    tail = jnp.maximum(q - total, 0)
    out_rb = jnp.where(qvalid, in_rb, nused + tail // n_ftiles)
    out_f = jnp.where(qvalid, in_f, tail % n_ftiles)
    qflags = qvalid.astype(I32) + 2 * (qvalid & (local % nbe == 0)).astype(I32)
    items = tuple(a.astype(I32) for a in (in_rb, qe, in_f, out_rb, out_f, qflags))
    zstart = jnp.maximum(bend * MOE_TM - MOE_TM, 0).astype(I32)
    npad = (nb * MOE_TM - counts).astype(I32)
    return dest, zstart, npad, nused.astype(I32).reshape(1), bc, be, bflags.astype(I32), items


def kernel(x, norm1_g, w_in, b_gates, w_pool_out, pool_scale, sgu_norm_g, w_spatial, b_spatial,
           w_sgu_out, w_out, norm2_g, w_router, b_router, w_gate, b_gate, w_up, b_up, w_down,
           b_down, normf_g):
    bsz, seq, d = x.shape
    n = bsz * seq
    groups, gd, _ = w_pool_out.shape
    pool_w = groups * gd
    sgu_w = w_sgu_out.shape[0]
    n_exp = w_router.shape[1]
    xf = x.reshape(n, d)

    w_in_b = w_in.astype(BF16)
    h = _rmsnorm_cast(xf, norm1_g)
    pooled = _inproj_pool(h, w_in_b, groups, gd, seq)
    s = _inproj_sgu(h, w_in_b, pool_w, pool_w + sgu_w, sgu_w, sgu_norm_g, w_spatial, b_spatial)
    m = _mix(h, w_in_b, pool_w + 2 * sgu_w, b_gates, pooled, w_pool_out.astype(BF16), pool_scale,
             s, w_sgu_out.astype(BF16))
    x1 = _outproj(m, w_out.astype(BF16), xf)

    h2, topi, rank, topw, counts = _router(x1, norm2_g, w_router, b_router)

    n_blocks = (n * TOP_K) // MOE_TM + n_exp
    n_ftiles = w_gate.shape[2] // MOE_TF
    dest, zstart, npad, nused, blk, blk_e, blk_flags, items = _routing_tables(
        counts[:, 0], topi, rank, n_blocks, n_ftiles)

    xs = _dispatch(zstart, npad, nused, dest, h2, n_blocks * MOE_TM)
    hm = _expert_up(items, xs, w_gate, w_up, b_gate, b_up)
    y = _expert_down(blk, blk_e, blk_flags, hm, w_down, b_down)
    dest_c = dest.transpose(0, 2, 1).reshape(n // COMB_TM, COMB_TM, TOP_K).transpose(0, 2, 1)
    out = _combine(dest_c, x1, topw.T, normf_g, y)
    return out.reshape(bsz, seq, d)
```

```python
import functools
import math

import jax
import jax.numpy as jnp
from jax import lax
from jax.experimental import pallas as pl
from jax.experimental.pallas import tpu as pltpu

CHUNK = 64
POOL_WINDOWS = (2, 4, 8, 16)
SGU_HEADS = 8
SGU_BLOCK = 128
TOP_K = 4
SWIGLU_LIMIT = 7.0
SWIGLU_ALPHA = 1.702
EPS = 1e-5

VMEM_LIMIT_BYTES = 56 * 1024 * 1024
POOL_HALO = 16
MOE_TM = 256
MOE_TF = 512
TOK_TM = 256
COMB_TM = 256
DMA_UNROLL = 8

BF16 = jnp.bfloat16
F32 = jnp.float32
I32 = jnp.int32


def _cparams(*sem):
    return pltpu.CompilerParams(dimension_semantics=sem, vmem_limit_bytes=VMEM_LIMIT_BYTES)


def _gelu_tanh(x):
    c = math.sqrt(2.0 / math.pi)
    return x * (0.5 * (1.0 + jnp.tanh(c * (x + 0.044715 * (x * x * x)))))


def _rms_scale(x):
    return lax.rsqrt(jnp.mean(x * x, axis=-1, keepdims=True) + EPS)


def _inproj_pool_kernel(x_ref, g_ref, w_ref, h_ref, o_ref, buf_ref, carry_ref, *, seq):
    i = pl.program_id(0)
    j = pl.program_id(1)
    tm = x_ref.shape[0]

    @pl.when(j == 0)
    def _():
        x = x_ref[...]
        h_ref[...] = (x * _rms_scale(x) * g_ref[...]).astype(h_ref.dtype)

    p = jnp.dot(h_ref[...], w_ref[...], preferred_element_type=F32)
    seq_pos0 = (i * tm) % seq
    buf_ref[0:POOL_HALO, :] = jnp.where(seq_pos0 == 0, 0.0, carry_ref[j])
    buf_ref[POOL_HALO:, :] = p
    carry_ref[j] = p[tm - POOL_HALO:, :]
    t1 = seq_pos0 + 1 + lax.broadcasted_iota(I32, (tm, 1), 0)

    for gi, win in enumerate(POOL_WINDOWS):
        @pl.when(j == gi)
        def _(win=win):
            s = buf_ref[...]
            shift = 1
            while shift < win:
                s = s + pltpu.roll(s, shift, 0)
                shift *= 2
            denom = jnp.minimum(t1, win).astype(F32)
            o_ref[...] = (s[POOL_HALO:, :] / denom - p).astype(o_ref.dtype)


def _inproj_pool(x, g, w, n_groups, gd, seq, tm=512):
    n, d = x.shape
    assert seq % tm == 0 and n_groups == len(POOL_WINDOWS) and max(POOL_WINDOWS) <= POOL_HALO
    return pl.pallas_call(
        functools.partial(_inproj_pool_kernel, seq=seq),
        out_shape=(jax.ShapeDtypeStruct((n, d), BF16),
                   jax.ShapeDtypeStruct((n, n_groups * gd), BF16)),
        grid=(n // tm, n_groups),
        in_specs=[pl.BlockSpec((tm, d), lambda i, j: (i, 0)),
                  pl.BlockSpec((1, d), lambda i, j: (0, 0)),
                  pl.BlockSpec((d, gd), lambda i, j: (0, j))],
        out_specs=(pl.BlockSpec((tm, d), lambda i, j: (i, 0)),
                   pl.BlockSpec((tm, gd), lambda i, j: (i, j))),
        scratch_shapes=[pltpu.VMEM((POOL_HALO + tm, gd), F32),
                        pltpu.VMEM((n_groups, POOL_HALO, gd), F32)],
        compiler_params=_cparams("arbitrary", "arbitrary"),
        name="inproj_pool",
    )(x, g.reshape(1, d), w)


def _inproj_sgu_kernel(h_ref, wu_ref, wv_ref, g_ref, ws_ref, b_ref, s_ref, *, head_dim):
    h = h_ref[...]
    u = _gelu_tanh(jnp.dot(h, wu_ref[...], preferred_element_type=F32))
    v = _gelu_tanh(jnp.dot(h, wv_ref[...], preferred_element_type=F32))
    pos_i = lax.broadcasted_iota(I32, (SGU_BLOCK, SGU_BLOCK), 0)
    pos_j = lax.broadcasted_iota(I32, (SGU_BLOCK, SGU_BLOCK), 1)
    readable = (pos_j // CHUNK) <= (pos_i // CHUNK)
    for hh in range(u.shape[1] // head_dim):
        cols = slice(hh * head_dim, (hh + 1) * head_dim)
        blk = v[:, cols]
        vn = (blk * _rms_scale(blk) * g_ref[:, cols]).astype(BF16)
        ws = jnp.where(readable, ws_ref[hh], 0.0).astype(BF16)
        bias = b_ref[hh]
        for c in range(u.shape[0] // SGU_BLOCK):
            rows = slice(c * SGU_BLOCK, (c + 1) * SGU_BLOCK)
            mixed = jnp.dot(ws, vn[rows, :], preferred_element_type=F32) + bias
            s_ref[rows, cols] = (u[rows, cols] * mixed).astype(s_ref.dtype)


def _inproj_sgu(h, w, u_col0, v_col0, width, norm_g, w_spatial, b_spatial, tm=1024, tn=512):
    n, d = h.shape
    hd = width // SGU_HEADS
    hpt = tn // hd
    return pl.pallas_call(
        functools.partial(_inproj_sgu_kernel, head_dim=hd),
        out_shape=jax.ShapeDtypeStruct((n, width), BF16),
        grid=(width // tn, n // tm),
        in_specs=[pl.BlockSpec((tm, d), lambda j, i: (i, 0)),
                  pl.BlockSpec((d, tn), lambda j, i: (0, j + u_col0 // tn)),
                  pl.BlockSpec((d, tn), lambda j, i: (0, j + v_col0 // tn)),
                  pl.BlockSpec((1, tn), lambda j, i: (0, j)),
                  pl.BlockSpec((hpt, SGU_BLOCK, SGU_BLOCK), lambda j, i: (j, 0, 0)),
                  pl.BlockSpec((hpt, SGU_BLOCK, 1), lambda j, i: (j, 0, 0))],
        out_specs=pl.BlockSpec((tm, tn), lambda j, i: (i, j)),
        compiler_params=_cparams("parallel", "parallel"),
        name="inproj_sgu",
    )(h, w, w, norm_g.reshape(1, width), w_spatial, b_spatial.reshape(SGU_HEADS, SGU_BLOCK, 1))


def _mix_kernel(h_ref, wga_ref, wgb_ref, ba_ref, bb_ref, pooled_ref, wp_ref, scale_ref,
                s_ref, wsgu_ref, m_ref):
    h = h_ref[...]
    g_a = jax.nn.sigmoid(jnp.dot(h, wga_ref[...], preferred_element_type=F32) + ba_ref[...])
    y_a = jnp.dot(pooled_ref[...], wp_ref[0], preferred_element_type=F32) * scale_ref[...]
    m = g_a * y_a
    g_b = jax.nn.sigmoid(jnp.dot(h, wgb_ref[...], preferred_element_type=F32) + bb_ref[...])
    y_b = jnp.dot(s_ref[...], wsgu_ref[...], preferred_element_type=F32)
    m_ref[...] = (m + g_b * y_b).astype(m_ref.dtype)


def _mix(h, w_in, gate_col0, b_gates, pooled, w_pool, pool_scale, s, w_sgu, tm=512, tn=512):
    n, d = h.shape
    groups, gd, od = w_pool.shape
    sw = s.shape[1]
    tpg = od // tn
    a0 = gate_col0 // tn
    nd = d // tn
    return pl.pallas_call(
        _mix_kernel,
        out_shape=jax.ShapeDtypeStruct((n, d), BF16),
        grid=(nd, n // tm),
        in_specs=[pl.BlockSpec((tm, d), lambda j, i: (i, 0)),
                  pl.BlockSpec((d, tn), lambda j, i: (0, j + a0)),
                  pl.BlockSpec((d, tn), lambda j, i: (0, j + a0 + nd)),
                  pl.BlockSpec((1, tn), lambda j, i: (0, j)),
                  pl.BlockSpec((1, tn), lambda j, i: (0, j + nd)),
                  pl.BlockSpec((tm, gd), lambda j, i: (i, j // tpg)),
                  pl.BlockSpec((1, gd, tn), lambda j, i: (j // tpg, 0, j % tpg)),
                  pl.BlockSpec((1, tn), lambda j, i: (0, j)),
                  pl.BlockSpec((tm, sw), lambda j, i: (i, 0)),
                  pl.BlockSpec((sw, tn), lambda j, i: (0, j))],
        out_specs=pl.BlockSpec((tm, tn), lambda j, i: (i, j)),
        compiler_params=_cparams("parallel", "parallel"),
        name="mix",
    )(h, w_in, w_in, b_gates.reshape(1, 2 * d), b_gates.reshape(1, 2 * d), pooled, w_pool,
      pool_scale.reshape(1, d), s, w_sgu)


def _outproj_kernel(m_ref, w_ref, x_ref, o_ref):
    o_ref[...] = x_ref[...] + jnp.dot(m_ref[...], w_ref[...], preferred_element_type=F32)


def _outproj(m, w, x, tm=512, tn=1024):
    n, d = m.shape
    return pl.pallas_call(
        _outproj_kernel,
        out_shape=jax.ShapeDtypeStruct((n, d), F32),
        grid=(d // tn, n // tm),
        in_specs=[pl.BlockSpec((tm, d), lambda j, i: (i, 0)),
                  pl.BlockSpec((d, tn), lambda j, i: (0, j)),
                  pl.BlockSpec((tm, tn), lambda j, i: (i, j))],
        out_specs=pl.BlockSpec((tm, tn), lambda j, i: (i, j)),
        compiler_params=_cparams("parallel", "parallel"),
        name="outproj",
    )(m, w, x)


def _pack_bf16_pair(lo, hi):
    lo_bits = lax.bitcast_convert_type(lo.astype(BF16).astype(F32), jnp.uint32)
    hi_bits = lax.bitcast_convert_type(hi.astype(BF16).astype(F32), jnp.uint32)
    return (hi_bits & jnp.uint32(0xFFFF0000)) | (lo_bits >> 16)


def _unpack_bf16_pair(words):
    lo = lax.bitcast_convert_type(words << 16, F32).astype(BF16)
    hi = lax.bitcast_convert_type(words & jnp.uint32(0xFFFF0000), F32).astype(BF16)
    return lo, hi


def _router_kernel(x_ref, g_ref, wr_ref, br_ref, h2_ref, topi_ref, rank_ref, topw_ref, cnt_ref,
                   carry_ref):
    i = pl.program_id(0)
    tm, d = x_ref.shape
    n_exp = wr_ref.shape[0]
    half = d // 2

    @pl.when(i == 0)
    def _():
        carry_ref[...] = jnp.zeros_like(carry_ref)

    x = x_ref[...]
    h2 = x * _rms_scale(x) * g_ref[...]
    h2_ref[...] = _pack_bf16_pair(h2[:, :half], h2[:, half:])

    logits = lax.dot_general(wr_ref[...].astype(BF16), h2.astype(BF16),
                             (((1,), (1,)), ((), ())), preferred_element_type=F32)
    logits = logits + br_ref[...]

    e_iota = lax.broadcasted_iota(I32, (n_exp, tm), 0)
    vals = logits
    top_v, sels = [], []
    for k in range(TOP_K):
        mx = jnp.max(vals, axis=0, keepdims=True)
        idx = jnp.min(jnp.where(vals == mx, e_iota, n_exp), axis=0, keepdims=True)
        sel = e_iota == idx
        vals = jnp.where(sel, -jnp.inf, vals)
        top_v.append(mx)
        sels.append(sel)
        topi_ref[0, k:k + 1, :] = idx

    exps = [jnp.exp(v - top_v[0]) for v in top_v]
    denom = exps[0] + exps[1] + exps[2] + exps[3]
    for k in range(TOP_K):
        topw_ref[k:k + 1, :] = exps[k] / denom

    chosen = jnp.zeros((n_exp, tm), F32)
    for sel in sels:
        chosen = chosen + sel.astype(F32)
    earlier = (lax.broadcasted_iota(I32, (tm, tm), 0)
               < lax.broadcasted_iota(I32, (tm, tm), 1)).astype(BF16)
    before = jnp.dot(chosen.astype(BF16), earlier, preferred_element_type=F32) + carry_ref[...]
    for k in range(TOP_K):
        r = jnp.sum(jnp.where(sels[k], before, 0.0), axis=0, keepdims=True)
        rank_ref[0, k:k + 1, :] = r.astype(I32)
    carry_ref[...] = carry_ref[...] + jnp.sum(chosen, axis=1, keepdims=True)
    cnt_ref[...] = carry_ref[...].astype(I32)


def _router(x1, g, w_router, b_router, tm=TOK_TM):
    n, d = x1.shape
    n_exp = w_router.shape[1]
    nt = n // tm
    return pl.pallas_call(
        _router_kernel,
        out_shape=(jax.ShapeDtypeStruct((n, d // 2), jnp.uint32),
                   jax.ShapeDtypeStruct((nt, TOP_K, tm), I32),
                   jax.ShapeDtypeStruct((nt, TOP_K, tm), I32),
                   jax.ShapeDtypeStruct((TOP_K, n), F32),
                   jax.ShapeDtypeStruct((n_exp, 1), I32)),
        grid=(nt,),
        in_specs=[pl.BlockSpec((tm, d), lambda i: (i, 0)),
                  pl.BlockSpec((1, d), lambda i: (0, 0)),
                  pl.BlockSpec((n_exp, d), lambda i: (0, 0)),
                  pl.BlockSpec((n_exp, 1), lambda i: (0, 0))],
        out_specs=(pl.BlockSpec((tm, d // 2), lambda i: (i, 0)),
                   pl.BlockSpec((1, TOP_K, tm), lambda i: (i, 0, 0)),
                   pl.BlockSpec((1, TOP_K, tm), lambda i: (i, 0, 0)),
                   pl.BlockSpec((TOP_K, tm), lambda i: (0, i)),
                   pl.BlockSpec((n_exp, 1), lambda i: (0, 0))),
        scratch_shapes=[pltpu.VMEM((n_exp, 1), F32)],
        compiler_params=_cparams("arbitrary"),
        name="router",
    )(x1, g.reshape(1, d), w_router.T, b_router.reshape(n_exp, 1))


def _dispatch_kernel(zstart_ref, npad_ref, nused_ref, dest_ref, h2_ref, xs_ref, zero_ref, sem):
    i = pl.program_id(0)
    tm = h2_ref.shape[0]
    n_exp = zstart_ref.shape[0]
    n_blocks = xs_ref.shape[0] // MOE_TM

    @pl.when(i == 0)
    def _():
        zero_ref[...] = jnp.zeros_like(zero_ref)

        def zero_copy(row0):
            return pltpu.make_async_copy(
                zero_ref, xs_ref.at[pl.ds(pl.multiple_of(row0, MOE_TM), MOE_TM)], sem)

        def pad_start(e, c):
            @pl.when(npad_ref[e] > 0)
            def _():
                zero_copy(zstart_ref[e]).start()
            return c

        def pad_wait(e, c):
            @pl.when(npad_ref[e] > 0)
            def _():
                zero_copy(zstart_ref[e]).wait()
            return c

        def tail_start(b, c):
            zero_copy(b * MOE_TM).start()
            return c

        def tail_wait(b, c):
            zero_copy(b * MOE_TM).wait()
            return c

        lax.fori_loop(0, n_exp, pad_start, 0)
        lax.fori_loop(nused_ref[0], n_blocks, tail_start, 0)
        lax.fori_loop(0, n_exp, pad_wait, 0)
        lax.fori_loop(nused_ref[0], n_blocks, tail_wait, 0)

    def start(t, c):
        for k in range(TOP_K):
            pltpu.make_async_copy(h2_ref.at[pl.ds(t, 1)],
                                  xs_ref.at[pl.ds(dest_ref[0, k, t], 1)], sem).start()
        return c

    def wait(t, c):
        for k in range(TOP_K):
            pltpu.make_async_copy(h2_ref.at[pl.ds(0, 1)], xs_ref.at[pl.ds(0, 1)], sem).wait()
        return c

    lax.fori_loop(0, tm, start, 0, unroll=DMA_UNROLL)
    lax.fori_loop(0, tm, wait, 0, unroll=DMA_UNROLL)


def _dispatch(zstart, npad, nused, dest, h2, n_rows, tm=TOK_TM):
    n, hw = h2.shape
    return pl.pallas_call(
        _dispatch_kernel,
        out_shape=jax.ShapeDtypeStruct((n_rows, hw), jnp.uint32),
        grid_spec=pltpu.PrefetchScalarGridSpec(
            num_scalar_prefetch=3,
            grid=(n // tm,),
            in_specs=[pl.BlockSpec((1, TOP_K, tm), lambda i, *_: (i, 0, 0), memory_space=pltpu.SMEM),
                      pl.BlockSpec((tm, hw), lambda i, *_: (i, 0))],
            out_specs=pl.BlockSpec(memory_space=pl.ANY),
            scratch_shapes=[pltpu.VMEM((MOE_TM, hw), jnp.uint32),
                            pltpu.SemaphoreType.DMA(())],
        ),
        compiler_params=_cparams("arbitrary"),
        name="dispatch",
    )(zstart, npad, nused, dest, h2)


FLAG_VALID, FLAG_FIRST, FLAG_HAS_NEXT = 1, 2, 4


def _expert_up_kernel(in_rb, in_e, in_f, out_rb, out_f, flags, nxt_e, nxt_f,
                      xs_ref, bg_ref, bu_ref, wg_hbm, wu_hbm, hm_ref,
                      stage_g, stage_u, wg_s, wu_s, sems):
    q = pl.program_id(0)
    fl = flags[q]

    def weight_copies(e, f):
        cols = pl.ds(pl.multiple_of(f * MOE_TF, MOE_TF), MOE_TF)
        return (pltpu.make_async_copy(wg_hbm.at[e, :, cols], stage_g, sems.at[0]),
                pltpu.make_async_copy(wu_hbm.at[e, :, cols], stage_u, sems.at[1]))

    @pl.when(q == 0)
    def _():
        for c in weight_copies(in_e[0], in_f[0]):
            c.start()

    @pl.when((fl & FLAG_FIRST) != 0)
    def _():
        for c in weight_copies(in_e[q], in_f[q]):
            c.wait()
        wg_s[...] = stage_g[...].astype(BF16)
        wu_s[...] = stage_u[...].astype(BF16)

        @pl.when((fl & FLAG_HAS_NEXT) != 0)
        def _():
            for c in weight_copies(nxt_e[q], nxt_f[q]):
                c.start()

    @pl.when((fl & FLAG_VALID) != 0)
    def _():
        half = xs_ref.shape[1]
        lo, hi = _unpack_bf16_pair(xs_ref[...])

        def proj(w_s, b_ref):
            return (jnp.dot(lo, w_s[:half, :], preferred_element_type=F32)
                    + jnp.dot(hi, w_s[half:, :], preferred_element_type=F32) + b_ref[0])

        gate = jnp.minimum(proj(wg_s, bg_ref), SWIGLU_LIMIT)
        up = jnp.clip(proj(wu_s, bu_ref), -SWIGLU_LIMIT, SWIGLU_LIMIT)
        glu = gate * jax.nn.sigmoid(SWIGLU_ALPHA * gate)
        hm_ref[...] = ((up + 1.0) * glu).astype(hm_ref.dtype)

    @pl.when((fl & FLAG_VALID) == 0)
    def _():
        hm_ref[...] = jnp.zeros_like(hm_ref)


def _expert_up(items, xs, wg, wu, bg, bu):
    n_rows, hw = xs.shape
    n_exp, d, f = wg.shape
    n_items = items[0].shape[0]
    b_spec = pl.BlockSpec((1, 1, MOE_TF), lambda q, rb, e, ft, *_: (e[q], 0, ft[q]))
    return pl.pallas_call(
        _expert_up_kernel,
        out_shape=jax.ShapeDtypeStruct((n_rows, f), BF16),
        grid_spec=pltpu.PrefetchScalarGridSpec(
            num_scalar_prefetch=len(items),
            grid=(n_items,),
            in_specs=[pl.BlockSpec((MOE_TM, hw), lambda q, rb, *_: (rb[q], 0)),
                      b_spec, b_spec,
                      pl.BlockSpec(memory_space=pl.ANY),
                      pl.BlockSpec(memory_space=pl.ANY)],
            out_specs=pl.BlockSpec((MOE_TM, MOE_TF),
                                   lambda q, rb, e, ft, orb, oft, *_: (orb[q], oft[q])),
            scratch_shapes=[pltpu.VMEM((d, MOE_TF), F32), pltpu.VMEM((d, MOE_TF), F32),
                            pltpu.VMEM((d, MOE_TF), BF16), pltpu.VMEM((d, MOE_TF), BF16),
                            pltpu.SemaphoreType.DMA((2,))],
        ),
        compiler_params=_cparams("arbitrary"),
        name="expert_up",
    )(*items, xs, bg.reshape(n_exp, 1, f), bu.reshape(n_exp, 1, f), wg, wu)


def _expert_down_kernel(bidx_ref, be_ref, flags, nxt_e, hm_ref, bd_ref, wd_hbm, y_ref,
                        stage, wd_s, sem):
    b = pl.program_id(0)
    fl = flags[b]
    half = y_ref.shape[1]

    def weight_copy(e):
        return pltpu.make_async_copy(wd_hbm.at[e], stage, sem)

    @pl.when(b == 0)
    def _():
        weight_copy(be_ref[0]).start()

    @pl.when((fl & FLAG_FIRST) != 0)
    def _():
        weight_copy(be_ref[b]).wait()
        wd_s[...] = stage[...].astype(BF16)

        @pl.when((fl & FLAG_HAS_NEXT) != 0)
        def _():
            weight_copy(nxt_e[b]).start()

    @pl.when((fl & FLAG_VALID) != 0)
    def _():
        y = jnp.dot(hm_ref[...], wd_s[...], preferred_element_type=F32) + bd_ref[0]
        y_ref[...] = _pack_bf16_pair(y[:, :half], y[:, half:])

    @pl.when((fl & FLAG_VALID) == 0)
    def _():
        y_ref[...] = jnp.zeros_like(y_ref)


def _expert_down(bidx, be, flags, nxt_e, hm, wd, bd):
    n_rows, f = hm.shape
    n_exp, _, d = wd.shape
    return pl.pallas_call(
        _expert_down_kernel,
        out_shape=jax.ShapeDtypeStruct((n_rows, d // 2), jnp.uint32),
        grid_spec=pltpu.PrefetchScalarGridSpec(
            num_scalar_prefetch=4,
            grid=(n_rows // MOE_TM,),
            in_specs=[pl.BlockSpec((MOE_TM, f), lambda b, bi, *_: (bi[b], 0)),
                      pl.BlockSpec((1, 1, d), lambda b, bi, be, *_: (be[b], 0, 0)),
                      pl.BlockSpec(memory_space=pl.ANY)],
            out_specs=pl.BlockSpec((MOE_TM, d // 2), lambda b, *_: (b, 0)),
            scratch_shapes=[pltpu.VMEM((f, d), F32), pltpu.VMEM((f, d), BF16),
                            pltpu.SemaphoreType.DMA(())],
        ),
        compiler_params=_cparams("arbitrary"),
        name="expert_down",
    )(bidx, be, flags, nxt_e, hm, bd.reshape(n_exp, 1, d), wd)


def _combine_kernel(dest_ref, dest_next_ref, x1_ref, w_ref, g_ref, y_ref, o_ref, buf_ref, sems):
    i = pl.program_id(0)
    nt = pl.num_programs(0)
    tm = x1_ref.shape[0]
    slot = i % 2

    def issue(d_ref, sl):
        def start(t, c):
            for k in range(TOP_K):
                pltpu.make_async_copy(y_ref.at[pl.ds(d_ref[0, k, t], 1)],
                                      buf_ref.at[sl, k, pl.ds(t, 1)], sems.at[sl]).start()
            return c
        lax.fori_loop(0, tm, start, 0, unroll=DMA_UNROLL)

    @pl.when(i == 0)
    def _():
        issue(dest_ref, 0)

    @pl.when(i + 1 < nt)
    def _():
        issue(dest_next_ref, 1 - slot)

    def wait(t, c):
        for k in range(TOP_K):
            pltpu.make_async_copy(y_ref.at[pl.ds(0, 1)], buf_ref.at[slot, 0, pl.ds(0, 1)],
                                  sems.at[slot]).wait()
        return c

    lax.fori_loop(0, tm, wait, 0, unroll=DMA_UNROLL)

    half = buf_ref.shape[-1]
    acc_lo = x1_ref[:, :half]
    acc_hi = x1_ref[:, half:]
    for k in range(TOP_K):
        words = buf_ref[slot, k]
        wk = w_ref[:, k:k + 1]
        acc_lo = acc_lo + wk * lax.bitcast_convert_type(words << 16, F32)
        acc_hi = acc_hi + wk * lax.bitcast_convert_type(words & jnp.uint32(0xFFFF0000), F32)
    ms = (jnp.sum(acc_lo * acc_lo, axis=-1, keepdims=True)
          + jnp.sum(acc_hi * acc_hi, axis=-1, keepdims=True)) / (2 * half)
    scale = lax.rsqrt(ms + EPS)
    o_ref[:, :half] = acc_lo * scale * g_ref[:, :half]
    o_ref[:, half:] = acc_hi * scale * g_ref[:, half:]


def _combine(dest, x1, topw_cols, g, y, tm=COMB_TM):
    n, d = x1.shape
    nt = n // tm
    return pl.pallas_call(
        _combine_kernel,
        out_shape=jax.ShapeDtypeStruct((n, d), F32),
        grid=(nt,),
        in_specs=[pl.BlockSpec((1, TOP_K, tm), lambda i: (i, 0, 0), memory_space=pltpu.SMEM),
                  pl.BlockSpec((1, TOP_K, tm), lambda i: (jnp.minimum(i + 1, nt - 1), 0, 0),
                               memory_space=pltpu.SMEM),
                  pl.BlockSpec((tm, d), lambda i: (i, 0)),
                  pl.BlockSpec((tm, TOP_K), lambda i: (i, 0)),
                  pl.BlockSpec((1, d), lambda i: (0, 0)),
                  pl.BlockSpec(memory_space=pl.ANY)],
        out_specs=pl.BlockSpec((tm, d), lambda i: (i, 0)),
        scratch_shapes=[pltpu.VMEM((2, TOP_K, tm, d // 2), jnp.uint32),
                        pltpu.SemaphoreType.DMA((2,))],
        compiler_params=_cparams("arbitrary"),
        name="combine",
    )(dest, dest, x1, topw_cols, g.reshape(1, d), y)


def _routing_tables(counts, topi, rank, n_blocks, n_ftiles):
    n_exp = counts.shape[0]
    nb = (counts + MOE_TM - 1) // MOE_TM
    bend = jnp.cumsum(nb)
    bstart = bend - nb
    nused = bend[-1]
    pstart = bstart * MOE_TM
    onehot = topi[..., None] == jnp.arange(n_exp, dtype=I32)
    dest = jnp.sum(jnp.where(onehot, pstart, 0), axis=-1).astype(I32) + rank
    b = jnp.arange(n_blocks, dtype=I32)
    bc = jnp.minimum(b, nused - 1)
    be = jnp.sum(bend[None, :] <= bc[:, None], axis=1).astype(I32)
    bvalid = b < nused
    bfirst = bvalid & (bc == bstart[be])
    bnext = b + nb[be]
    b_nxt_e = be[jnp.minimum(bnext, n_blocks - 1)]
    bflags = (FLAG_VALID * bvalid.astype(I32) + FLAG_FIRST * bfirst.astype(I32)
              + FLAG_HAS_NEXT * (bfirst & (bnext < nused)).astype(I32))
    q = jnp.arange(n_ftiles * n_blocks, dtype=I32)
    total = n_ftiles * nused
    qc = jnp.minimum(q, total - 1)
    qe = jnp.sum((n_ftiles * bend)[None, :] <= qc[:, None], axis=1).astype(I32)
    local = qc - n_ftiles * bstart[qe]
    nbe = jnp.maximum(nb[qe], 1)
    in_f = local // nbe
    in_rb = bstart[qe] + local % nbe
    qvalid = q < total
    tail = jnp.maximum(q - total, 0)
    out_rb = jnp.where(qvalid, in_rb, nused + tail // n_ftiles)
    out_f = jnp.where(qvalid, in_f, tail % n_ftiles)
    qfirst = qvalid & (local % nbe == 0)
    qnext = q + nbe
    qn = jnp.minimum(qnext, q.shape[0] - 1)
    qflags = (FLAG_VALID * qvalid.astype(I32) + FLAG_FIRST * qfirst.astype(I32)
              + FLAG_HAS_NEXT * (qfirst & (qnext < total)).astype(I32))
    items = tuple(a.astype(I32) for a in (in_rb, qe, in_f, out_rb, out_f, qflags, qe[qn], in_f[qn]))
    zstart = jnp.maximum(bend * MOE_TM - MOE_TM, 0).astype(I32)
    npad = (nb * MOE_TM - counts).astype(I32)
    blocks = tuple(a.astype(I32) for a in (bc, be, bflags, b_nxt_e))
    return dest, zstart, npad, nused.astype(I32).reshape(1), blocks, items


def kernel(x, norm1_g, w_in, b_gates, w_pool_out, pool_scale, sgu_norm_g, w_spatial, b_spatial,
           w_sgu_out, w_out, norm2_g, w_router, b_router, w_gate, b_gate, w_up, b_up, w_down,
           b_down, normf_g):
    bsz, seq, d = x.shape
    n = bsz * seq
    groups, gd, _ = w_pool_out.shape
    pool_w = groups * gd
    sgu_w = w_sgu_out.shape[0]
    n_exp = w_router.shape[1]
    xf = x.reshape(n, d)

    w_in_b = w_in.astype(BF16)
    h, pooled = _inproj_pool(xf, norm1_g, w_in_b, groups, gd, seq)
    s = _inproj_sgu(h, w_in_b, pool_w, pool_w + sgu_w, sgu_w, sgu_norm_g, w_spatial, b_spatial)
    m = _mix(h, w_in_b, pool_w + 2 * sgu_w, b_gates, pooled, w_pool_out.astype(BF16), pool_scale,
             s, w_sgu_out.astype(BF16))
    x1 = _outproj(m, w_out.astype(BF16), xf)

    h2, topi, rank, topw, counts = _router(x1, norm2_g, w_router, b_router)

    n_blocks = (n * TOP_K) // MOE_TM + n_exp
    n_ftiles = w_gate.shape[2] // MOE_TF
    dest, zstart, npad, nused, blocks, items = _routing_tables(
        counts[:, 0], topi, rank, n_blocks, n_ftiles)

    xs = _dispatch(zstart, npad, nused, dest, h2, n_blocks * MOE_TM)
    hm = _expert_up(items, xs, w_gate, w_up, b_gate, b_up)
    y = _expert_down(*blocks, hm, w_down, b_down)
    assert COMB_TM == TOK_TM
    out = _combine(dest, x1, topw.T, normf_g, y)
    return out.reshape(bsz, seq, d)
```

```python
import functools
import math

import jax
import jax.numpy as jnp
from jax import lax
from jax.experimental import pallas as pl
from jax.experimental.pallas import tpu as pltpu

CHUNK = 64
POOL_WINDOWS = (2, 4, 8, 16)
SGU_HEADS = 8
SGU_BLOCK = 128
TOP_K = 4
SWIGLU_LIMIT = 7.0
SWIGLU_ALPHA = 1.702
EPS = 1e-5

VMEM_LIMIT_BYTES = 58 * 1024 * 1024
POOL_HALO = 16
MOE_TM = 512
MOE_TF = 512
TOK_TM = 256
COMB_TM = 256
DMA_UNROLL = 8

BF16 = jnp.bfloat16
F32 = jnp.float32
I32 = jnp.int32


def _cparams(*sem):
    return pltpu.CompilerParams(dimension_semantics=sem, vmem_limit_bytes=VMEM_LIMIT_BYTES)


def _gelu_tanh(x):
    c = math.sqrt(2.0 / math.pi)
    return x * (0.5 * (1.0 + jnp.tanh(c * (x + 0.044715 * (x * x * x)))))


def _rms_scale(x):
    return lax.rsqrt(jnp.mean(x * x, axis=-1, keepdims=True) + EPS)


def _inproj_pool_kernel(x_ref, g_ref, w_ref, h_ref, o_ref, buf_ref, carry_ref, *, seq):
    i = pl.program_id(0)
    tm = x_ref.shape[0]
    gd = buf_ref.shape[-1]
    x = x_ref[...]
    h = (x * _rms_scale(x) * g_ref[...]).astype(h_ref.dtype)
    h_ref[...] = h
    seq_pos0 = (i * tm) % seq
    t1 = seq_pos0 + 1 + lax.broadcasted_iota(I32, (tm, 1), 0)

    for gi, win in enumerate(POOL_WINDOWS):
        cols = slice(gi * gd, (gi + 1) * gd)
        p = jnp.dot(h, w_ref[:, cols], preferred_element_type=F32)
        buf_ref[gi, 0:POOL_HALO, :] = jnp.where(seq_pos0 == 0, 0.0, carry_ref[gi])
        buf_ref[gi, POOL_HALO:, :] = p
        carry_ref[gi] = p[tm - POOL_HALO:, :]
        s = buf_ref[gi]
        shift = 1
        while shift < win:
            s = s + pltpu.roll(s, shift, 0)
            shift *= 2
        denom = jnp.minimum(t1, win).astype(F32)
        o_ref[:, cols] = (s[POOL_HALO:, :] / denom - p).astype(o_ref.dtype)


def _inproj_pool(x, g, w, n_groups, gd, seq, tm=512):
    n, d = x.shape
    assert seq % tm == 0 and n_groups == len(POOL_WINDOWS) and max(POOL_WINDOWS) <= POOL_HALO
    pw = n_groups * gd
    return pl.pallas_call(
        functools.partial(_inproj_pool_kernel, seq=seq),
        out_shape=(jax.ShapeDtypeStruct((n, d), BF16),
                   jax.ShapeDtypeStruct((n, pw), BF16)),
        grid=(n // tm,),
        in_specs=[pl.BlockSpec((tm, d), lambda i: (i, 0)),
                  pl.BlockSpec((1, d), lambda i: (0, 0)),
                  pl.BlockSpec((d, pw), lambda i: (0, 0), pipeline_mode=pl.Buffered(1))],
        out_specs=(pl.BlockSpec((tm, d), lambda i: (i, 0)),
                   pl.BlockSpec((tm, pw), lambda i: (i, 0))),
        scratch_shapes=[pltpu.VMEM((n_groups, POOL_HALO + tm, gd), F32),
                        pltpu.VMEM((n_groups, POOL_HALO, gd), F32)],
        compiler_params=_cparams("arbitrary"),
        name="inproj_pool",
    )(x, g.reshape(1, d), w)


def _stage_weight_tiles(w_hbm, col0s, tn, stages, casts, sems):
    j = pl.program_id(0)
    i = pl.program_id(1)

    def copies(jj):
        return [pltpu.make_async_copy(
                    w_hbm.at[:, pl.ds(pl.multiple_of(c0 + jj * tn, tn), tn)], st, sems.at[k])
                for k, (c0, st) in enumerate(zip(col0s, stages))]

    @pl.when((j == 0) & (i == 0))
    def _():
        for c in copies(0):
            c.start()

    @pl.when(i == 0)
    def _():
        for c in copies(j):
            c.wait()
        for st, dst in zip(stages, casts):
            dst[...] = st[...].astype(dst.dtype)

        @pl.when(j + 1 < pl.num_programs(0))
        def _():
            for c in copies(j + 1):
                c.start()


def _inproj_sgu_kernel(h_ref, g_ref, ws_ref, b_ref, w_hbm, s_ref, stage_u, stage_v, wu_ref, wv_ref,
                       sems, *, head_dim, u_col0, v_col0):
    _stage_weight_tiles(w_hbm, (u_col0, v_col0), s_ref.shape[1], (stage_u, stage_v),
                        (wu_ref, wv_ref), sems)
    h = h_ref[...]
    u = _gelu_tanh(jnp.dot(h, wu_ref[...], preferred_element_type=F32))
    v = _gelu_tanh(jnp.dot(h, wv_ref[...], preferred_element_type=F32))
    pos_i = lax.broadcasted_iota(I32, (SGU_BLOCK, SGU_BLOCK), 0)
    pos_j = lax.broadcasted_iota(I32, (SGU_BLOCK, SGU_BLOCK), 1)
    readable = (pos_j // CHUNK) <= (pos_i // CHUNK)
    for hh in range(u.shape[1] // head_dim):
        cols = slice(hh * head_dim, (hh + 1) * head_dim)
        blk = v[:, cols]
        vn = (blk * _rms_scale(blk) * g_ref[:, cols]).astype(BF16)
        ws = jnp.where(readable, ws_ref[hh], 0.0).astype(BF16)
        bias = b_ref[hh]
        for c in range(u.shape[0] // SGU_BLOCK):
            rows = slice(c * SGU_BLOCK, (c + 1) * SGU_BLOCK)
            mixed = jnp.dot(ws, vn[rows, :], preferred_element_type=F32) + bias
            s_ref[rows, cols] = (u[rows, cols] * mixed).astype(s_ref.dtype)


def _inproj_sgu(h, w, u_col0, v_col0, width, norm_g, w_spatial, b_spatial, tm=1024, tn=512):
    n, d = h.shape
    hd = width // SGU_HEADS
    hpt = tn // hd
    assert u_col0 % tn == 0 and v_col0 % tn == 0
    return pl.pallas_call(
        functools.partial(_inproj_sgu_kernel, head_dim=hd, u_col0=u_col0, v_col0=v_col0),
        out_shape=jax.ShapeDtypeStruct((n, width), BF16),
        grid=(width // tn, n // tm),
        in_specs=[pl.BlockSpec((tm, d), lambda j, i: (i, 0)),
                  pl.BlockSpec((1, tn), lambda j, i: (0, j)),
                  pl.BlockSpec((hpt, SGU_BLOCK, SGU_BLOCK), lambda j, i: (j, 0, 0)),
                  pl.BlockSpec((hpt, SGU_BLOCK, 1), lambda j, i: (j, 0, 0)),
                  pl.BlockSpec(memory_space=pl.ANY)],
        out_specs=pl.BlockSpec((tm, tn), lambda j, i: (i, j)),
        scratch_shapes=[pltpu.VMEM((d, tn), F32), pltpu.VMEM((d, tn), F32),
                        pltpu.VMEM((d, tn), BF16), pltpu.VMEM((d, tn), BF16),
                        pltpu.SemaphoreType.DMA((2,))],
        compiler_params=_cparams("arbitrary", "arbitrary"),
        name="inproj_sgu",
    )(h, norm_g.reshape(1, width), w_spatial, b_spatial.reshape(SGU_HEADS, SGU_BLOCK, 1), w)


def _mix_kernel(h_ref, ba_ref, bb_ref, pooled_ref, wp_ref, scale_ref, s_ref, wsgu_ref, w_hbm,
                m_ref, stage_a, stage_b, wga_ref, wgb_ref, sems, *, ga_col0, gb_col0):
    _stage_weight_tiles(w_hbm, (ga_col0, gb_col0), m_ref.shape[1], (stage_a, stage_b),
                        (wga_ref, wgb_ref), sems)
    h = h_ref[...]
    g_a = jax.nn.sigmoid(jnp.dot(h, wga_ref[...], preferred_element_type=F32) + ba_ref[...])
    y_a = jnp.dot(pooled_ref[...], wp_ref[0], preferred_element_type=F32) * scale_ref[...]
    m = g_a * y_a
    g_b = jax.nn.sigmoid(jnp.dot(h, wgb_ref[...], preferred_element_type=F32) + bb_ref[...])
    y_b = jnp.dot(s_ref[...], wsgu_ref[...], preferred_element_type=F32)
    m_ref[...] = (m + g_b * y_b).astype(m_ref.dtype)


def _mix(h, w_in, gate_col0, b_gates, pooled, w_pool, pool_scale, s, w_sgu, tm=512, tn=512):
    n, d = h.shape
    groups, gd, od = w_pool.shape
    sw = s.shape[1]
    tpg = od // tn
    nd = d // tn
    assert gate_col0 % tn == 0
    return pl.pallas_call(
        functools.partial(_mix_kernel, ga_col0=gate_col0, gb_col0=gate_col0 + d),
        out_shape=jax.ShapeDtypeStruct((n, d), BF16),
        grid=(nd, n // tm),
        in_specs=[pl.BlockSpec((tm, d), lambda j, i: (i, 0)),
                  pl.BlockSpec((1, tn), lambda j, i: (0, j)),
                  pl.BlockSpec((1, tn), lambda j, i: (0, j + nd)),
                  pl.BlockSpec((tm, gd), lambda j, i: (i, j // tpg)),
                  pl.BlockSpec((1, gd, tn), lambda j, i: (j // tpg, 0, j % tpg)),
                  pl.BlockSpec((1, tn), lambda j, i: (0, j)),
                  pl.BlockSpec((tm, sw), lambda j, i: (i, 0)),
                  pl.BlockSpec((sw, tn), lambda j, i: (0, j)),
                  pl.BlockSpec(memory_space=pl.ANY)],
        out_specs=pl.BlockSpec((tm, tn), lambda j, i: (i, j)),
        scratch_shapes=[pltpu.VMEM((d, tn), F32), pltpu.VMEM((d, tn), F32),
                        pltpu.VMEM((d, tn), BF16), pltpu.VMEM((d, tn), BF16),
                        pltpu.SemaphoreType.DMA((2,))],
        compiler_params=_cparams("arbitrary", "arbitrary"),
        name="mix",
    )(h, b_gates.reshape(1, 2 * d), b_gates.reshape(1, 2 * d), pooled, w_pool,
      pool_scale.reshape(1, d), s, w_sgu, w_in)


def _outproj_kernel(m_ref, w_ref, x_ref, o_ref):
    o_ref[...] = x_ref[...] + jnp.dot(m_ref[...], w_ref[...], preferred_element_type=F32)


def _outproj(m, w, x, tm=512, tn=1024):
    n, d = m.shape
    return pl.pallas_call(
        _outproj_kernel,
        out_shape=jax.ShapeDtypeStruct((n, d), F32),
        grid=(d // tn, n // tm),
        in_specs=[pl.BlockSpec((tm, d), lambda j, i: (i, 0)),
                  pl.BlockSpec((d, tn), lambda j, i: (0, j)),
                  pl.BlockSpec((tm, tn), lambda j, i: (i, j))],
        out_specs=pl.BlockSpec((tm, tn), lambda j, i: (i, j)),
        compiler_params=_cparams("parallel", "parallel"),
        name="outproj",
    )(m, w, x)


def _pack_bf16_pair(lo, hi):
    lo_bits = lax.bitcast_convert_type(lo.astype(BF16).astype(F32), jnp.uint32)
    hi_bits = lax.bitcast_convert_type(hi.astype(BF16).astype(F32), jnp.uint32)
    return (hi_bits & jnp.uint32(0xFFFF0000)) | (lo_bits >> 16)


def _unpack_bf16_pair(words):
    lo = lax.bitcast_convert_type(words << 16, F32).astype(BF16)
    hi = lax.bitcast_convert_type(words & jnp.uint32(0xFFFF0000), F32).astype(BF16)
    return lo, hi


def _router_kernel(x_ref, g_ref, wr_ref, br_ref, h2_ref, topi_ref, rank_ref, topw_ref, cnt_ref,
                   carry_ref):
    i = pl.program_id(0)
    tm, d = x_ref.shape
    n_exp = wr_ref.shape[0]
    half = d // 2

    @pl.when(i == 0)
    def _():
        carry_ref[...] = jnp.zeros_like(carry_ref)

    x = x_ref[...]
    h2 = x * _rms_scale(x) * g_ref[...]
    h2_ref[...] = _pack_bf16_pair(h2[:, :half], h2[:, half:])

    logits = lax.dot_general(wr_ref[...].astype(BF16), h2.astype(BF16),
                             (((1,), (1,)), ((), ())), preferred_element_type=F32)
    logits = logits + br_ref[...]

    e_iota = lax.broadcasted_iota(I32, (n_exp, tm), 0)
    vals = logits
    top_v, sels = [], []
    for k in range(TOP_K):
        mx = jnp.max(vals, axis=0, keepdims=True)
        idx = jnp.min(jnp.where(vals == mx, e_iota, n_exp), axis=0, keepdims=True)
        sel = e_iota == idx
        vals = jnp.where(sel, -jnp.inf, vals)
        top_v.append(mx)
        sels.append(sel)
        topi_ref[0, k:k + 1, :] = idx

    exps = [jnp.exp(v - top_v[0]) for v in top_v]
    denom = exps[0] + exps[1] + exps[2] + exps[3]
    for k in range(TOP_K):
        topw_ref[k:k + 1, :] = exps[k] / denom

    chosen = jnp.zeros((n_exp, tm), F32)
    for sel in sels:
        chosen = chosen + sel.astype(F32)
    earlier = (lax.broadcasted_iota(I32, (tm, tm), 0)
               < lax.broadcasted_iota(I32, (tm, tm), 1)).astype(BF16)
    before = jnp.dot(chosen.astype(BF16), earlier, preferred_element_type=F32) + carry_ref[...]
    for k in range(TOP_K):
        r = jnp.sum(jnp.where(sels[k], before, 0.0), axis=0, keepdims=True)
        rank_ref[0, k:k + 1, :] = r.astype(I32)
    carry_ref[...] = carry_ref[...] + jnp.sum(chosen, axis=1, keepdims=True)
    cnt_ref[...] = carry_ref[...].astype(I32)


def _router(x1, g, w_router, b_router, tm=TOK_TM):
    n, d = x1.shape
    n_exp = w_router.shape[1]
    nt = n // tm
    return pl.pallas_call(
        _router_kernel,
        out_shape=(jax.ShapeDtypeStruct((n, d // 2), jnp.uint32),
                   jax.ShapeDtypeStruct((nt, TOP_K, tm), I32),
                   jax.ShapeDtypeStruct((nt, TOP_K, tm), I32),
                   jax.ShapeDtypeStruct((TOP_K, n), F32),
                   jax.ShapeDtypeStruct((n_exp, 1), I32)),
        grid=(nt,),
        in_specs=[pl.BlockSpec((tm, d), lambda i: (i, 0)),
                  pl.BlockSpec((1, d), lambda i: (0, 0)),
                  pl.BlockSpec((n_exp, d), lambda i: (0, 0)),
                  pl.BlockSpec((n_exp, 1), lambda i: (0, 0))],
        out_specs=(pl.BlockSpec((tm, d // 2), lambda i: (i, 0)),
                   pl.BlockSpec((1, TOP_K, tm), lambda i: (i, 0, 0)),
                   pl.BlockSpec((1, TOP_K, tm), lambda i: (i, 0, 0)),
                   pl.BlockSpec((TOP_K, tm), lambda i: (0, i)),
                   pl.BlockSpec((n_exp, 1), lambda i: (0, 0))),
        scratch_shapes=[pltpu.VMEM((n_exp, 1), F32)],
        compiler_params=_cparams("arbitrary"),
        name="router",
    )(x1, g.reshape(1, d), w_router.T, b_router.reshape(n_exp, 1))


def _dispatch_kernel(zstart_ref, npad_ref, nused_ref, dest_ref, h2_ref, xs_ref, zero_ref, sem):
    i = pl.program_id(0)
    tm = h2_ref.shape[0]
    n_exp = zstart_ref.shape[0]
    n_blocks = xs_ref.shape[0] // MOE_TM

    @pl.when(i == 0)
    def _():
        zero_ref[...] = jnp.zeros_like(zero_ref)

        def zero_copy(row0):
            return pltpu.make_async_copy(
                zero_ref, xs_ref.at[pl.ds(pl.multiple_of(row0, MOE_TM), MOE_TM)], sem)

        def pad_start(e, c):
            @pl.when(npad_ref[e] > 0)
            def _():
                zero_copy(zstart_ref[e]).start()
            return c

        def pad_wait(e, c):
            @pl.when(npad_ref[e] > 0)
            def _():
                zero_copy(zstart_ref[e]).wait()
            return c

        def tail_start(b, c):
            zero_copy(b * MOE_TM).start()
            return c

        def tail_wait(b, c):
            zero_copy(b * MOE_TM).wait()
            return c

        lax.fori_loop(0, n_exp, pad_start, 0)
        lax.fori_loop(nused_ref[0], n_blocks, tail_start, 0)
        lax.fori_loop(0, n_exp, pad_wait, 0)
        lax.fori_loop(nused_ref[0], n_blocks, tail_wait, 0)

    def start(t, c):
        for k in range(TOP_K):
            pltpu.make_async_copy(h2_ref.at[pl.ds(t, 1)],
                                  xs_ref.at[pl.ds(dest_ref[0, k, t], 1)], sem).start()
        return c

    def wait(t, c):
        for k in range(TOP_K):
            pltpu.make_async_copy(h2_ref.at[pl.ds(0, 1)], xs_ref.at[pl.ds(0, 1)], sem).wait()
        return c

    lax.fori_loop(0, tm, start, 0, unroll=DMA_UNROLL)
    lax.fori_loop(0, tm, wait, 0, unroll=DMA_UNROLL)


def _dispatch(zstart, npad, nused, dest, h2, n_rows, tm=TOK_TM):
    n, hw = h2.shape
    return pl.pallas_call(
        _dispatch_kernel,
        out_shape=jax.ShapeDtypeStruct((n_rows, hw), jnp.uint32),
        grid_spec=pltpu.PrefetchScalarGridSpec(
            num_scalar_prefetch=3,
            grid=(n // tm,),
            in_specs=[pl.BlockSpec((1, TOP_K, tm), lambda i, *_: (i, 0, 0), memory_space=pltpu.SMEM),
                      pl.BlockSpec((tm, hw), lambda i, *_: (i, 0))],
            out_specs=pl.BlockSpec(memory_space=pl.ANY),
            scratch_shapes=[pltpu.VMEM((MOE_TM, hw), jnp.uint32),
                            pltpu.SemaphoreType.DMA(())],
        ),
        compiler_params=_cparams("arbitrary"),
        name="dispatch",
    )(zstart, npad, nused, dest, h2)


FLAG_VALID, FLAG_FIRST, FLAG_HAS_NEXT = 1, 2, 4


def _expert_up_kernel(in_rb, in_e, in_f, out_rb, out_f, flags, nxt_e, nxt_f,
                      xs_ref, bg_ref, bu_ref, wg_hbm, wu_hbm, hm_ref,
                      stage_g, stage_u, wg_s, wu_s, sems):
    q = pl.program_id(0)
    fl = flags[q]

    def weight_copies(e, f):
        cols = pl.ds(pl.multiple_of(f * MOE_TF, MOE_TF), MOE_TF)
        return (pltpu.make_async_copy(wg_hbm.at[e, :, cols], stage_g, sems.at[0]),
                pltpu.make_async_copy(wu_hbm.at[e, :, cols], stage_u, sems.at[1]))

    @pl.when(q == 0)
    def _():
        for c in weight_copies(in_e[0], in_f[0]):
            c.start()

    @pl.when((fl & FLAG_FIRST) != 0)
    def _():
        for c in weight_copies(in_e[q], in_f[q]):
            c.wait()
        wg_s[...] = stage_g[...].astype(BF16)
        wu_s[...] = stage_u[...].astype(BF16)

        @pl.when((fl & FLAG_HAS_NEXT) != 0)
        def _():
            for c in weight_copies(nxt_e[q], nxt_f[q]):
                c.start()

    @pl.when((fl & FLAG_VALID) != 0)
    def _():
        half = xs_ref.shape[1]
        lo, hi = _unpack_bf16_pair(xs_ref[...])

        def proj(w_s, b_ref):
            return (jnp.dot(lo, w_s[:half, :], preferred_element_type=F32)
                    + jnp.dot(hi, w_s[half:, :], preferred_element_type=F32) + b_ref[0])

        gate = jnp.minimum(proj(wg_s, bg_ref), SWIGLU_LIMIT)
        up = jnp.clip(proj(wu_s, bu_ref), -SWIGLU_LIMIT, SWIGLU_LIMIT)
        glu = gate * jax.nn.sigmoid(SWIGLU_ALPHA * gate)
        hm_ref[...] = ((up + 1.0) * glu).astype(hm_ref.dtype)

    @pl.when((fl & FLAG_VALID) == 0)
    def _():
        hm_ref[...] = jnp.zeros_like(hm_ref)


def _expert_up(items, xs, wg, wu, bg, bu):
    n_rows, hw = xs.shape
    n_exp, d, f = wg.shape
    n_items = items[0].shape[0]
    b_spec = pl.BlockSpec((1, 1, MOE_TF), lambda q, rb, e, ft, *_: (e[q], 0, ft[q]))
    return pl.pallas_call(
        _expert_up_kernel,
        out_shape=jax.ShapeDtypeStruct((n_rows, f), BF16),
        grid_spec=pltpu.PrefetchScalarGridSpec(
            num_scalar_prefetch=len(items),
            grid=(n_items,),
            in_specs=[pl.BlockSpec((MOE_TM, hw), lambda q, rb, *_: (rb[q], 0)),
                      b_spec, b_spec,
                      pl.BlockSpec(memory_space=pl.ANY),
                      pl.BlockSpec(memory_space=pl.ANY)],
            out_specs=pl.BlockSpec((MOE_TM, MOE_TF),
                                   lambda q, rb, e, ft, orb, oft, *_: (orb[q], oft[q])),
            scratch_shapes=[pltpu.VMEM((d, MOE_TF), F32), pltpu.VMEM((d, MOE_TF), F32),
                            pltpu.VMEM((d, MOE_TF), BF16), pltpu.VMEM((d, MOE_TF), BF16),
                            pltpu.SemaphoreType.DMA((2,))],
        ),
        compiler_params=_cparams("arbitrary"),
        name="expert_up",
    )(*items, xs, bg.reshape(n_exp, 1, f), bu.reshape(n_exp, 1, f), wg, wu)


def _expert_down_kernel(bidx_ref, be_ref, flags, nxt_e, hm_ref, bd_ref, wd_hbm, y_ref,
                        stage, wd_s, sem):
    b = pl.program_id(0)
    fl = flags[b]
    half = y_ref.shape[1]

    def weight_copy(e):
        return pltpu.make_async_copy(wd_hbm.at[e], stage, sem)

    @pl.when(b == 0)
    def _():
        weight_copy(be_ref[0]).start()

    @pl.when((fl & FLAG_FIRST) != 0)
    def _():
        weight_copy(be_ref[b]).wait()
        wd_s[...] = stage[...].astype(BF16)

        @pl.when((fl & FLAG_HAS_NEXT) != 0)
        def _():
            weight_copy(nxt_e[b]).start()

    @pl.when((fl & FLAG_VALID) != 0)
    def _():
        y = jnp.dot(hm_ref[...], wd_s[...], preferred_element_type=F32) + bd_ref[0]
        y_ref[...] = _pack_bf16_pair(y[:, :half], y[:, half:])

    @pl.when((fl & FLAG_VALID) == 0)
    def _():
        y_ref[...] = jnp.zeros_like(y_ref)


def _expert_down(bidx, be, flags, nxt_e, hm, wd, bd):
    n_rows, f = hm.shape
    n_exp, _, d = wd.shape
    return pl.pallas_call(
        _expert_down_kernel,
        out_shape=jax.ShapeDtypeStruct((n_rows, d // 2), jnp.uint32),
        grid_spec=pltpu.PrefetchScalarGridSpec(
            num_scalar_prefetch=4,
            grid=(n_rows // MOE_TM,),
            in_specs=[pl.BlockSpec((MOE_TM, f), lambda b, bi, *_: (bi[b], 0)),
                      pl.BlockSpec((1, 1, d), lambda b, bi, be, *_: (be[b], 0, 0)),
                      pl.BlockSpec(memory_space=pl.ANY)],
            out_specs=pl.BlockSpec((MOE_TM, d // 2), lambda b, *_: (b, 0)),
            scratch_shapes=[pltpu.VMEM((f, d), F32), pltpu.VMEM((f, d), BF16),
                            pltpu.SemaphoreType.DMA(())],
        ),
        compiler_params=_cparams("arbitrary"),
        name="expert_down",
    )(bidx, be, flags, nxt_e, hm, bd.reshape(n_exp, 1, d), wd)


def _combine_kernel(dest_ref, dest_next_ref, x1_ref, w_ref, g_ref, y_ref, o_ref, buf_ref, sems):
    i = pl.program_id(0)
    nt = pl.num_programs(0)
    tm = x1_ref.shape[0]
    slot = i % 2

    def issue(d_ref, sl):
        def start(t, c):
            for k in range(TOP_K):
                pltpu.make_async_copy(y_ref.at[pl.ds(d_ref[0, k, t], 1)],
                                      buf_ref.at[sl, k, pl.ds(t, 1)], sems.at[sl]).start()
            return c
        lax.fori_loop(0, tm, start, 0, unroll=DMA_UNROLL)

    @pl.when(i == 0)
    def _():
        issue(dest_ref, 0)

    @pl.when(i + 1 < nt)
    def _():
        issue(dest_next_ref, 1 - slot)

    def wait(t, c):
        for k in range(TOP_K):
            pltpu.make_async_copy(y_ref.at[pl.ds(0, 1)], buf_ref.at[slot, 0, pl.ds(0, 1)],
                                  sems.at[slot]).wait()
        return c

    lax.fori_loop(0, tm, wait, 0, unroll=DMA_UNROLL)

    half = buf_ref.shape[-1]
    acc_lo = x1_ref[:, :half]
    acc_hi = x1_ref[:, half:]
    for k in range(TOP_K):
        words = buf_ref[slot, k]
        wk = w_ref[:, k:k + 1]
        acc_lo = acc_lo + wk * lax.bitcast_convert_type(words << 16, F32)
        acc_hi = acc_hi + wk * lax.bitcast_convert_type(words & jnp.uint32(0xFFFF0000), F32)
    ms = (jnp.sum(acc_lo * acc_lo, axis=-1, keepdims=True)
          + jnp.sum(acc_hi * acc_hi, axis=-1, keepdims=True)) / (2 * half)
    scale = lax.rsqrt(ms + EPS)
    o_ref[:, :half] = acc_lo * scale * g_ref[:, :half]
    o_ref[:, half:] = acc_hi * scale * g_ref[:, half:]


def _combine(dest, x1, topw_cols, g, y, tm=COMB_TM):
    n, d = x1.shape
    nt = n // tm
    return pl.pallas_call(
        _combine_kernel,
        out_shape=jax.ShapeDtypeStruct((n, d), F32),
        grid=(nt,),
        in_specs=[pl.BlockSpec((1, TOP_K, tm), lambda i: (i, 0, 0), memory_space=pltpu.SMEM),
                  pl.BlockSpec((1, TOP_K, tm), lambda i: (jnp.minimum(i + 1, nt - 1), 0, 0),
                               memory_space=pltpu.SMEM),
                  pl.BlockSpec((tm, d), lambda i: (i, 0)),
                  pl.BlockSpec((tm, TOP_K), lambda i: (i, 0)),
                  pl.BlockSpec((1, d), lambda i: (0, 0)),
                  pl.BlockSpec(memory_space=pl.ANY)],
        out_specs=pl.BlockSpec((tm, d), lambda i: (i, 0)),
        scratch_shapes=[pltpu.VMEM((2, TOP_K, tm, d // 2), jnp.uint32),
                        pltpu.SemaphoreType.DMA((2,))],
        compiler_params=_cparams("arbitrary"),
        name="combine",
    )(dest, dest, x1, topw_cols, g.reshape(1, d), y)


def _routing_tables(counts, topi, rank, n_blocks, n_ftiles):
    n_exp = counts.shape[0]
    nb = (counts + MOE_TM - 1) // MOE_TM
    bend = jnp.cumsum(nb)
    bstart = bend - nb
    nused = bend[-1]
    pstart = bstart * MOE_TM
    onehot = topi[..., None] == jnp.arange(n_exp, dtype=I32)
    dest = jnp.sum(jnp.where(onehot, pstart, 0), axis=-1).astype(I32) + rank
    b = jnp.arange(n_blocks, dtype=I32)
    bc = jnp.minimum(b, nused - 1)
    be = jnp.sum(bend[None, :] <= bc[:, None], axis=1).astype(I32)
    bvalid = b < nused
    bfirst = bvalid & (bc == bstart[be])
    bnext = b + nb[be]
    b_nxt_e = be[jnp.minimum(bnext, n_blocks - 1)]
    bflags = (FLAG_VALID * bvalid.astype(I32) + FLAG_FIRST * bfirst.astype(I32)
              + FLAG_HAS_NEXT * (bfirst & (bnext < nused)).astype(I32))
    q = jnp.arange(n_ftiles * n_blocks, dtype=I32)
    total = n_ftiles * nused
    qc = jnp.minimum(q, total - 1)
    qe = jnp.sum((n_ftiles * bend)[None, :] <= qc[:, None], axis=1).astype(I32)
    local = qc - n_ftiles * bstart[qe]
    nbe = jnp.maximum(nb[qe], 1)
    in_f = local // nbe
    in_rb = bstart[qe] + local % nbe
    qvalid = q < total
    tail = jnp.maximum(q - total, 0)
    out_rb = jnp.where(qvalid, in_rb, nused + tail // n_ftiles)
    out_f = jnp.where(qvalid, in_f, tail % n_ftiles)
    qfirst = qvalid & (local % nbe == 0)
    qnext = q + nbe
    qn = jnp.minimum(qnext, q.shape[0] - 1)
    qflags = (FLAG_VALID * qvalid.astype(I32) + FLAG_FIRST * qfirst.astype(I32)
              + FLAG_HAS_NEXT * (qfirst & (qnext < total)).astype(I32))
    items = tuple(a.astype(I32) for a in (in_rb, qe, in_f, out_rb, out_f, qflags, qe[qn], in_f[qn]))
    zstart = jnp.maximum(bend * MOE_TM - MOE_TM, 0).astype(I32)
    npad = (nb * MOE_TM - counts).astype(I32)
    blocks = tuple(a.astype(I32) for a in (bc, be, bflags, b_nxt_e))
    return dest, zstart, npad, nused.astype(I32).reshape(1), blocks, items


def kernel(x, norm1_g, w_in, b_gates, w_pool_out, pool_scale, sgu_norm_g, w_spatial, b_spatial,
           w_sgu_out, w_out, norm2_g, w_router, b_router, w_gate, b_gate, w_up, b_up, w_down,
           b_down, normf_g):
    bsz, seq, d = x.shape
    n = bsz * seq
    groups, gd, _ = w_pool_out.shape
    pool_w = groups * gd
    sgu_w = w_sgu_out.shape[0]
    n_exp = w_router.shape[1]
    xf = x.reshape(n, d)

    h, pooled = _inproj_pool(xf, norm1_g, w_in[:, :pool_w].astype(BF16), groups, gd, seq)
    s = _inproj_sgu(h, w_in, pool_w, pool_w + sgu_w, sgu_w, sgu_norm_g, w_spatial, b_spatial)
    m = _mix(h, w_in, pool_w + 2 * sgu_w, b_gates, pooled, w_pool_out.astype(BF16), pool_scale,
             s, w_sgu_out.astype(BF16))
    x1 = _outproj(m, w_out.astype(BF16), xf)

    h2, topi, rank, topw, counts = _router(x1, norm2_g, w_router, b_router)

    n_blocks = (n * TOP_K) // MOE_TM + n_exp
    n_ftiles = w_gate.shape[2] // MOE_TF
    dest, zstart, npad, nused, blocks, items = _routing_tables(
        counts[:, 0], topi, rank, n_blocks, n_ftiles)

    xs = _dispatch(zstart, npad, nused, dest, h2, n_blocks * MOE_TM)
    hm = _expert_up(items, xs, w_gate, w_up, b_gate, b_up)
    y = _expert_down(*blocks, hm, w_down, b_down)
    assert COMB_TM == TOK_TM
    out = _combine(dest, x1, topw.T, normf_g, y)
    return out.reshape(bsz, seq, d)
```

```python
import functools
import math

import jax
import jax.numpy as jnp
from jax import lax
from jax.experimental import pallas as pl
from jax.experimental.pallas import tpu as pltpu

CHUNK = 64
POOL_WINDOWS = (2, 4, 8, 16)
SGU_HEADS = 8
SGU_BLOCK = 128
TOP_K = 4
SWIGLU_LIMIT = 7.0
SWIGLU_ALPHA = 1.702
EPS = 1e-5

VMEM_LIMIT_BYTES = 58 * 1024 * 1024
POOL_HALO = 16
MOE_TM = 512
MOE_TF = 512
TOK_TM = 256
COMB_TM = 256
DMA_UNROLL = 8

BF16 = jnp.bfloat16
F32 = jnp.float32
I32 = jnp.int32


def _cparams(*sem):
    return pltpu.CompilerParams(dimension_semantics=sem, vmem_limit_bytes=VMEM_LIMIT_BYTES)


def _gelu_tanh(x):
    c = math.sqrt(2.0 / math.pi)
    return x * (0.5 * (1.0 + jnp.tanh(c * (x + 0.044715 * (x * x * x)))))


def _rms_scale(x):
    return lax.rsqrt(jnp.mean(x * x, axis=-1, keepdims=True) + EPS)


def _inproj_pool_kernel(x_ref, g_ref, w_ref, h_ref, o_ref, buf_ref, carry_ref, *, seq):
    i = pl.program_id(0)
    tm = x_ref.shape[0]
    gd = buf_ref.shape[-1]
    x = x_ref[...]
    h = (x * _rms_scale(x) * g_ref[...]).astype(h_ref.dtype)
    h_ref[...] = h
    seq_pos0 = (i * tm) % seq
    t1 = seq_pos0 + 1 + lax.broadcasted_iota(I32, (tm, 1), 0)

    for gi, win in enumerate(POOL_WINDOWS):
        cols = slice(gi * gd, (gi + 1) * gd)
        p = jnp.dot(h, w_ref[:, cols], preferred_element_type=F32)
        buf_ref[gi, 0:POOL_HALO, :] = jnp.where(seq_pos0 == 0, 0.0, carry_ref[gi])
        buf_ref[gi, POOL_HALO:, :] = p
        carry_ref[gi] = p[tm - POOL_HALO:, :]
        s = buf_ref[gi]
        shift = 1
        while shift < win:
            s = s + pltpu.roll(s, shift, 0)
            shift *= 2
        denom = jnp.minimum(t1, win).astype(F32)
        o_ref[:, cols] = (s[POOL_HALO:, :] / denom - p).astype(o_ref.dtype)


def _inproj_pool(x, g, w, n_groups, gd, seq, tm=512):
    n, d = x.shape
    assert seq % tm == 0 and n_groups == len(POOL_WINDOWS) and max(POOL_WINDOWS) <= POOL_HALO
    pw = n_groups * gd
    return pl.pallas_call(
        functools.partial(_inproj_pool_kernel, seq=seq),
        out_shape=(jax.ShapeDtypeStruct((n, d), BF16),
                   jax.ShapeDtypeStruct((n, pw), BF16)),
        grid=(n // tm,),
        in_specs=[pl.BlockSpec((tm, d), lambda i: (i, 0)),
                  pl.BlockSpec((1, d), lambda i: (0, 0)),
                  pl.BlockSpec((d, pw), lambda i: (0, 0), pipeline_mode=pl.Buffered(1))],
        out_specs=(pl.BlockSpec((tm, d), lambda i: (i, 0)),
                   pl.BlockSpec((tm, pw), lambda i: (i, 0))),
        scratch_shapes=[pltpu.VMEM((n_groups, POOL_HALO + tm, gd), F32),
                        pltpu.VMEM((n_groups, POOL_HALO, gd), F32)],
        compiler_params=_cparams("arbitrary"),
        name="inproj_pool",
    )(x, g.reshape(1, d), w)


def _stage_weight_tiles(w_hbm, col0s, tn, stages, casts, sems):
    j = pl.program_id(0)
    i = pl.program_id(1)

    def copies(jj):
        return [pltpu.make_async_copy(
                    w_hbm.at[:, pl.ds(pl.multiple_of(c0 + jj * tn, tn), tn)], st, sems.at[k])
                for k, (c0, st) in enumerate(zip(col0s, stages))]

    @pl.when((j == 0) & (i == 0))
    def _():
        for c in copies(0):
            c.start()

    @pl.when(i == 0)
    def _():
        for c in copies(j):
            c.wait()
        for st, dst in zip(stages, casts):
            dst[...] = st[...].astype(dst.dtype)

        @pl.when(j + 1 < pl.num_programs(0))
        def _():
            for c in copies(j + 1):
                c.start()


def _inproj_sgu_kernel(h_ref, g_ref, ws_ref, b_ref, w_hbm, s_ref, stage_u, stage_v, wu_ref, wv_ref,
                       sems, *, head_dim, u_col0, v_col0):
    _stage_weight_tiles(w_hbm, (u_col0, v_col0), s_ref.shape[1], (stage_u, stage_v),
                        (wu_ref, wv_ref), sems)
    h = h_ref[...]
    u = _gelu_tanh(jnp.dot(h, wu_ref[...], preferred_element_type=F32))
    v = _gelu_tanh(jnp.dot(h, wv_ref[...], preferred_element_type=F32))
    pos_i = lax.broadcasted_iota(I32, (SGU_BLOCK, SGU_BLOCK), 0)
    pos_j = lax.broadcasted_iota(I32, (SGU_BLOCK, SGU_BLOCK), 1)
    readable = (pos_j // CHUNK) <= (pos_i // CHUNK)
    for hh in range(u.shape[1] // head_dim):
        cols = slice(hh * head_dim, (hh + 1) * head_dim)
        blk = v[:, cols]
        vn = (blk * _rms_scale(blk) * g_ref[:, cols]).astype(BF16)
        ws = jnp.where(readable, ws_ref[hh], 0.0).astype(BF16)
        bias = b_ref[hh]
        for c in range(u.shape[0] // SGU_BLOCK):
            rows = slice(c * SGU_BLOCK, (c + 1) * SGU_BLOCK)
            mixed = jnp.dot(ws, vn[rows, :], preferred_element_type=F32) + bias
            s_ref[rows, cols] = (u[rows, cols] * mixed).astype(s_ref.dtype)


def _inproj_sgu(h, w, u_col0, v_col0, width, norm_g, w_spatial, b_spatial, tm=1024, tn=512):
    n, d = h.shape
    hd = width // SGU_HEADS
    hpt = tn // hd
    assert u_col0 % tn == 0 and v_col0 % tn == 0
    return pl.pallas_call(
        functools.partial(_inproj_sgu_kernel, head_dim=hd, u_col0=u_col0, v_col0=v_col0),
        out_shape=jax.ShapeDtypeStruct((n, width), BF16),
        grid=(width // tn, n // tm),
        in_specs=[pl.BlockSpec((tm, d), lambda j, i: (i, 0)),
                  pl.BlockSpec((1, tn), lambda j, i: (0, j)),
                  pl.BlockSpec((hpt, SGU_BLOCK, SGU_BLOCK), lambda j, i: (j, 0, 0)),
                  pl.BlockSpec((hpt, SGU_BLOCK, 1), lambda j, i: (j, 0, 0)),
                  pl.BlockSpec(memory_space=pl.ANY)],
        out_specs=pl.BlockSpec((tm, tn), lambda j, i: (i, j)),
        scratch_shapes=[pltpu.VMEM((d, tn), F32), pltpu.VMEM((d, tn), F32),
                        pltpu.VMEM((d, tn), BF16), pltpu.VMEM((d, tn), BF16),
                        pltpu.SemaphoreType.DMA((2,))],
        compiler_params=_cparams("arbitrary", "arbitrary"),
        name="inproj_sgu",
    )(h, norm_g.reshape(1, width), w_spatial, b_spatial.reshape(SGU_HEADS, SGU_BLOCK, 1), w)


def _mix_kernel(h_ref, ba_ref, bb_ref, pooled_ref, wp_ref, scale_ref, s_ref, wsgu_ref, w_hbm,
                m_ref, stage_a, stage_b, wga_ref, wgb_ref, sems, *, ga_col0, gb_col0):
    _stage_weight_tiles(w_hbm, (ga_col0, gb_col0), m_ref.shape[1], (stage_a, stage_b),
                        (wga_ref, wgb_ref), sems)
    h = h_ref[...]
    g_a = jax.nn.sigmoid(jnp.dot(h, wga_ref[...], preferred_element_type=F32) + ba_ref[...])
    y_a = jnp.dot(pooled_ref[...], wp_ref[0], preferred_element_type=F32) * scale_ref[...]
    m = g_a * y_a
    g_b = jax.nn.sigmoid(jnp.dot(h, wgb_ref[...], preferred_element_type=F32) + bb_ref[...])
    y_b = jnp.dot(s_ref[...], wsgu_ref[...], preferred_element_type=F32)
    m_ref[...] = (m + g_b * y_b).astype(m_ref.dtype)


def _mix(h, w_in, gate_col0, b_gates, pooled, w_pool, pool_scale, s, w_sgu, tm=512, tn=512):
    n, d = h.shape
    groups, gd, od = w_pool.shape
    sw = s.shape[1]
    tpg = od // tn
    nd = d // tn
    assert gate_col0 % tn == 0
    return pl.pallas_call(
        functools.partial(_mix_kernel, ga_col0=gate_col0, gb_col0=gate_col0 + d),
        out_shape=jax.ShapeDtypeStruct((n, d), BF16),
        grid=(nd, n // tm),
        in_specs=[pl.BlockSpec((tm, d), lambda j, i: (i, 0)),
                  pl.BlockSpec((1, tn), lambda j, i: (0, j)),
                  pl.BlockSpec((1, tn), lambda j, i: (0, j + nd)),
                  pl.BlockSpec((tm, gd), lambda j, i: (i, j // tpg)),
                  pl.BlockSpec((1, gd, tn), lambda j, i: (j // tpg, 0, j % tpg)),
                  pl.BlockSpec((1, tn), lambda j, i: (0, j)),
                  pl.BlockSpec((tm, sw), lambda j, i: (i, 0)),
                  pl.BlockSpec((sw, tn), lambda j, i: (0, j)),
                  pl.BlockSpec(memory_space=pl.ANY)],
        out_specs=pl.BlockSpec((tm, tn), lambda j, i: (i, j)),
        scratch_shapes=[pltpu.VMEM((d, tn), F32), pltpu.VMEM((d, tn), F32),
                        pltpu.VMEM((d, tn), BF16), pltpu.VMEM((d, tn), BF16),
                        pltpu.SemaphoreType.DMA((2,))],
        compiler_params=_cparams("arbitrary", "arbitrary"),
        name="mix",
    )(h, b_gates.reshape(1, 2 * d), b_gates.reshape(1, 2 * d), pooled, w_pool,
      pool_scale.reshape(1, d), s, w_sgu, w_in)


def _outproj_kernel(m_ref, x_ref, w_hbm, o_ref, stage, w_ref, sems):
    _stage_weight_tiles(w_hbm, (0,), o_ref.shape[1], (stage,), (w_ref,), sems)
    o_ref[...] = x_ref[...] + jnp.dot(m_ref[...], w_ref[...], preferred_element_type=F32)


def _outproj(m, w, x, tm=512, tn=1024):
    n, d = m.shape
    return pl.pallas_call(
        _outproj_kernel,
        out_shape=jax.ShapeDtypeStruct((n, d), F32),
        grid=(d // tn, n // tm),
        in_specs=[pl.BlockSpec((tm, d), lambda j, i: (i, 0)),
                  pl.BlockSpec((tm, tn), lambda j, i: (i, j)),
                  pl.BlockSpec(memory_space=pl.ANY)],
        out_specs=pl.BlockSpec((tm, tn), lambda j, i: (i, j)),
        scratch_shapes=[pltpu.VMEM((d, tn), F32), pltpu.VMEM((d, tn), BF16),
                        pltpu.SemaphoreType.DMA((1,))],
        compiler_params=_cparams("arbitrary", "arbitrary"),
        name="outproj",
    )(m, x, w)


def _pack_bf16_pair(lo, hi):
    lo_bits = lax.bitcast_convert_type(lo.astype(BF16).astype(F32), jnp.uint32)
    hi_bits = lax.bitcast_convert_type(hi.astype(BF16).astype(F32), jnp.uint32)
    return (hi_bits & jnp.uint32(0xFFFF0000)) | (lo_bits >> 16)


def _unpack_bf16_pair(words):
    lo = lax.bitcast_convert_type(words << 16, F32).astype(BF16)
    hi = lax.bitcast_convert_type(words & jnp.uint32(0xFFFF0000), F32).astype(BF16)
    return lo, hi


def _router_kernel(x_ref, g_ref, wr_ref, br_ref, h2_ref, topi_ref, rank_ref, topw_ref, cnt_ref,
                   carry_ref):
    i = pl.program_id(0)
    tm, d = x_ref.shape
    n_exp = wr_ref.shape[0]
    half = d // 2

    @pl.when(i == 0)
    def _():
        carry_ref[...] = jnp.zeros_like(carry_ref)

    x = x_ref[...]
    h2 = x * _rms_scale(x) * g_ref[...]
    h2_ref[...] = _pack_bf16_pair(h2[:, :half], h2[:, half:])

    logits = lax.dot_general(wr_ref[...].astype(BF16), h2.astype(BF16),
                             (((1,), (1,)), ((), ())), preferred_element_type=F32)
    logits = logits + br_ref[...]

    e_iota = lax.broadcasted_iota(I32, (n_exp, tm), 0)
    vals = logits
    top_v, sels = [], []
    for k in range(TOP_K):
        mx = jnp.max(vals, axis=0, keepdims=True)
        idx = jnp.min(jnp.where(vals == mx, e_iota, n_exp), axis=0, keepdims=True)
        sel = e_iota == idx
        vals = jnp.where(sel, -jnp.inf, vals)
        top_v.append(mx)
        sels.append(sel)
        topi_ref[0, k:k + 1, :] = idx

    exps = [jnp.exp(v - top_v[0]) for v in top_v]
    denom = exps[0] + exps[1] + exps[2] + exps[3]
    for k in range(TOP_K):
        topw_ref[k:k + 1, :] = exps[k] / denom

    chosen = jnp.zeros((n_exp, tm), F32)
    for sel in sels:
        chosen = chosen + sel.astype(F32)
    earlier = (lax.broadcasted_iota(I32, (tm, tm), 0)
               < lax.broadcasted_iota(I32, (tm, tm), 1)).astype(BF16)
    before = jnp.dot(chosen.astype(BF16), earlier, preferred_element_type=F32) + carry_ref[...]
    for k in range(TOP_K):
        r = jnp.sum(jnp.where(sels[k], before, 0.0), axis=0, keepdims=True)
        rank_ref[0, k:k + 1, :] = r.astype(I32)
    carry_ref[...] = carry_ref[...] + jnp.sum(chosen, axis=1, keepdims=True)
    cnt_ref[...] = carry_ref[...].astype(I32)


def _router(x1, g, w_router, b_router, tm=TOK_TM):
    n, d = x1.shape
    n_exp = w_router.shape[1]
    nt = n // tm
    return pl.pallas_call(
        _router_kernel,
        out_shape=(jax.ShapeDtypeStruct((n, d // 2), jnp.uint32),
                   jax.ShapeDtypeStruct((nt, TOP_K, tm), I32),
                   jax.ShapeDtypeStruct((nt, TOP_K, tm), I32),
                   jax.ShapeDtypeStruct((TOP_K, n), F32),
                   jax.ShapeDtypeStruct((n_exp, 1), I32)),
        grid=(nt,),
        in_specs=[pl.BlockSpec((tm, d), lambda i: (i, 0)),
                  pl.BlockSpec((1, d), lambda i: (0, 0)),
                  pl.BlockSpec((n_exp, d), lambda i: (0, 0)),
                  pl.BlockSpec((n_exp, 1), lambda i: (0, 0))],
        out_specs=(pl.BlockSpec((tm, d // 2), lambda i: (i, 0)),
                   pl.BlockSpec((1, TOP_K, tm), lambda i: (i, 0, 0)),
                   pl.BlockSpec((1, TOP_K, tm), lambda i: (i, 0, 0)),
                   pl.BlockSpec((TOP_K, tm), lambda i: (0, i)),
                   pl.BlockSpec((n_exp, 1), lambda i: (0, 0))),
        scratch_shapes=[pltpu.VMEM((n_exp, 1), F32)],
        compiler_params=_cparams("arbitrary"),
        name="router",
    )(x1, g.reshape(1, d), w_router.T, b_router.reshape(n_exp, 1))


def _dispatch_kernel(zstart_ref, npad_ref, nused_ref, dest_ref, h2_ref, xs_ref, zero_ref, sem):
    i = pl.program_id(0)
    tm = h2_ref.shape[0]
    n_exp = zstart_ref.shape[0]
    n_blocks = xs_ref.shape[0] // MOE_TM

    @pl.when(i == 0)
    def _():
        zero_ref[...] = jnp.zeros_like(zero_ref)

        def zero_copy(row0):
            return pltpu.make_async_copy(
                zero_ref, xs_ref.at[pl.ds(pl.multiple_of(row0, MOE_TM), MOE_TM)], sem)

        def pad_start(e, c):
            @pl.when(npad_ref[e] > 0)
            def _():
                zero_copy(zstart_ref[e]).start()
            return c

        def pad_wait(e, c):
            @pl.when(npad_ref[e] > 0)
            def _():
                zero_copy(zstart_ref[e]).wait()
            return c

        def tail_start(b, c):
            zero_copy(b * MOE_TM).start()
            return c

        def tail_wait(b, c):
            zero_copy(b * MOE_TM).wait()
            return c

        lax.fori_loop(0, n_exp, pad_start, 0)
        lax.fori_loop(nused_ref[0], n_blocks, tail_start, 0)
        lax.fori_loop(0, n_exp, pad_wait, 0)
        lax.fori_loop(nused_ref[0], n_blocks, tail_wait, 0)

    def start(t, c):
        for k in range(TOP_K):
            pltpu.make_async_copy(h2_ref.at[pl.ds(t, 1)],
                                  xs_ref.at[pl.ds(dest_ref[0, k, t], 1)], sem).start(priority=k % 2)
        return c

    def wait(t, c):
        for k in range(TOP_K):
            pltpu.make_async_copy(h2_ref.at[pl.ds(0, 1)], xs_ref.at[pl.ds(0, 1)], sem).wait()
        return c

    lax.fori_loop(0, tm, start, 0, unroll=DMA_UNROLL)
    lax.fori_loop(0, tm, wait, 0, unroll=DMA_UNROLL)


def _dispatch(zstart, npad, nused, dest, h2, n_rows, tm=TOK_TM):
    n, hw = h2.shape
    return pl.pallas_call(
        _dispatch_kernel,
        out_shape=jax.ShapeDtypeStruct((n_rows, hw), jnp.uint32),
        grid_spec=pltpu.PrefetchScalarGridSpec(
            num_scalar_prefetch=3,
            grid=(n // tm,),
            in_specs=[pl.BlockSpec((1, TOP_K, tm), lambda i, *_: (i, 0, 0), memory_space=pltpu.SMEM),
                      pl.BlockSpec((tm, hw), lambda i, *_: (i, 0))],
            out_specs=pl.BlockSpec(memory_space=pl.ANY),
            scratch_shapes=[pltpu.VMEM((MOE_TM, hw), jnp.uint32),
                            pltpu.SemaphoreType.DMA(())],
        ),
        compiler_params=_cparams("arbitrary"),
        name="dispatch",
    )(zstart, npad, nused, dest, h2)


FLAG_VALID, FLAG_FIRST, FLAG_HAS_NEXT = 1, 2, 4


def _expert_up_kernel(in_rb, in_e, in_f, out_rb, out_f, flags, nxt_e, nxt_f,
                      xs_ref, bg_ref, bu_ref, wg_hbm, wu_hbm, hm_ref,
                      stage_g, stage_u, wg_s, wu_s, sems):
    q = pl.program_id(0)
    fl = flags[q]

    def weight_copies(e, f):
        cols = pl.ds(pl.multiple_of(f * MOE_TF, MOE_TF), MOE_TF)
        return (pltpu.make_async_copy(wg_hbm.at[e, :, cols], stage_g, sems.at[0]),
                pltpu.make_async_copy(wu_hbm.at[e, :, cols], stage_u, sems.at[1]))

    @pl.when(q == 0)
    def _():
        for c in weight_copies(in_e[0], in_f[0]):
            c.start()

    @pl.when((fl & FLAG_FIRST) != 0)
    def _():
        for c in weight_copies(in_e[q], in_f[q]):
            c.wait()
        wg_s[...] = stage_g[...].astype(BF16)
        wu_s[...] = stage_u[...].astype(BF16)

        @pl.when((fl & FLAG_HAS_NEXT) != 0)
        def _():
            for c in weight_copies(nxt_e[q], nxt_f[q]):
                c.start()

    @pl.when((fl & FLAG_VALID) != 0)
    def _():
        half = xs_ref.shape[1]
        lo, hi = _unpack_bf16_pair(xs_ref[...])

        def proj(w_s, b_ref):
            return (jnp.dot(lo, w_s[:half, :], preferred_element_type=F32)
                    + jnp.dot(hi, w_s[half:, :], preferred_element_type=F32) + b_ref[0])

        gate = jnp.minimum(proj(wg_s, bg_ref), SWIGLU_LIMIT)
        up = jnp.clip(proj(wu_s, bu_ref), -SWIGLU_LIMIT, SWIGLU_LIMIT)
        glu = gate * jax.nn.sigmoid(SWIGLU_ALPHA * gate)
        hm_ref[...] = ((up + 1.0) * glu).astype(hm_ref.dtype)

    @pl.when((fl & FLAG_VALID) == 0)
    def _():
        hm_ref[...] = jnp.zeros_like(hm_ref)


def _expert_up(items, xs, wg, wu, bg, bu):
    n_rows, hw = xs.shape
    n_exp, d, f = wg.shape
    n_items = items[0].shape[0]
    b_spec = pl.BlockSpec((1, 1, MOE_TF), lambda q, rb, e, ft, *_: (e[q], 0, ft[q]))
    return pl.pallas_call(
        _expert_up_kernel,
        out_shape=jax.ShapeDtypeStruct((n_rows, f), BF16),
        grid_spec=pltpu.PrefetchScalarGridSpec(
            num_scalar_prefetch=len(items),
            grid=(n_items,),
            in_specs=[pl.BlockSpec((MOE_TM, hw), lambda q, rb, *_: (rb[q], 0)),
                      b_spec, b_spec,
                      pl.BlockSpec(memory_space=pl.ANY),
                      pl.BlockSpec(memory_space=pl.ANY)],
            out_specs=pl.BlockSpec((MOE_TM, MOE_TF),
                                   lambda q, rb, e, ft, orb, oft, *_: (orb[q], oft[q])),
            scratch_shapes=[pltpu.VMEM((d, MOE_TF), F32), pltpu.VMEM((d, MOE_TF), F32),
                            pltpu.VMEM((d, MOE_TF), BF16), pltpu.VMEM((d, MOE_TF), BF16),
                            pltpu.SemaphoreType.DMA((2,))],
        ),
        compiler_params=_cparams("arbitrary"),
        name="expert_up",
    )(*items, xs, bg.reshape(n_exp, 1, f), bu.reshape(n_exp, 1, f), wg, wu)


def _expert_down_kernel(bidx_ref, be_ref, flags, nxt_e, hm_ref, bd_ref, wd_hbm, y_ref,
                        stage, wd_s, sem):
    b = pl.program_id(0)
    fl = flags[b]
    half = y_ref.shape[1]

    def weight_copy(e):
        return pltpu.make_async_copy(wd_hbm.at[e], stage, sem)

    @pl.when(b == 0)
    def _():
        weight_copy(be_ref[0]).start()

    @pl.when((fl & FLAG_FIRST) != 0)
    def _():
        weight_copy(be_ref[b]).wait()
        wd_s[...] = stage[...].astype(BF16)

        @pl.when((fl & FLAG_HAS_NEXT) != 0)
        def _():
            weight_copy(nxt_e[b]).start()

    @pl.when((fl & FLAG_VALID) != 0)
    def _():
        y = jnp.dot(hm_ref[...], wd_s[...], preferred_element_type=F32) + bd_ref[0]
        y_ref[...] = _pack_bf16_pair(y[:, :half], y[:, half:])

    @pl.when((fl & FLAG_VALID) == 0)
    def _():
        y_ref[...] = jnp.zeros_like(y_ref)


def _expert_down(bidx, be, flags, nxt_e, hm, wd, bd):
    n_rows, f = hm.shape
    n_exp, _, d = wd.shape
    return pl.pallas_call(
        _expert_down_kernel,
        out_shape=jax.ShapeDtypeStruct((n_rows, d // 2), jnp.uint32),
        grid_spec=pltpu.PrefetchScalarGridSpec(
            num_scalar_prefetch=4,
            grid=(n_rows // MOE_TM,),
            in_specs=[pl.BlockSpec((MOE_TM, f), lambda b, bi, *_: (bi[b], 0)),
                      pl.BlockSpec((1, 1, d), lambda b, bi, be, *_: (be[b], 0, 0)),
                      pl.BlockSpec(memory_space=pl.ANY)],
            out_specs=pl.BlockSpec((MOE_TM, d // 2), lambda b, *_: (b, 0)),
            scratch_shapes=[pltpu.VMEM((f, d), F32), pltpu.VMEM((f, d), BF16),
                            pltpu.SemaphoreType.DMA(())],
        ),
        compiler_params=_cparams("arbitrary"),
        name="expert_down",
    )(bidx, be, flags, nxt_e, hm, bd.reshape(n_exp, 1, d), wd)


def _combine_kernel(dest_ref, dest_next_ref, x1_ref, w_ref, g_ref, y_ref, o_ref, buf_ref, sems):
    i = pl.program_id(0)
    nt = pl.num_programs(0)
    tm = x1_ref.shape[0]
    slot = i % 2

    def issue(d_ref, sl):
        def start(t, c):
            for k in range(TOP_K):
                pltpu.make_async_copy(y_ref.at[pl.ds(d_ref[0, k, t], 1)],
                                      buf_ref.at[sl, k, pl.ds(t, 1)], sems.at[sl]).start(priority=k % 2)
            return c
        lax.fori_loop(0, tm, start, 0, unroll=DMA_UNROLL)

    @pl.when(i == 0)
    def _():
        issue(dest_ref, 0)

    @pl.when(i + 1 < nt)
    def _():
        issue(dest_next_ref, 1 - slot)

    def wait(t, c):
        for k in range(TOP_K):
            pltpu.make_async_copy(y_ref.at[pl.ds(0, 1)], buf_ref.at[slot, 0, pl.ds(0, 1)],
                                  sems.at[slot]).wait()
        return c

    lax.fori_loop(0, tm, wait, 0, unroll=DMA_UNROLL)

    half = buf_ref.shape[-1]
    acc_lo = x1_ref[:, :half]
    acc_hi = x1_ref[:, half:]
    for k in range(TOP_K):
        words = buf_ref[slot, k]
        wk = w_ref[:, k:k + 1]
        acc_lo = acc_lo + wk * lax.bitcast_convert_type(words << 16, F32)
        acc_hi = acc_hi + wk * lax.bitcast_convert_type(words & jnp.uint32(0xFFFF0000), F32)
    ms = (jnp.sum(acc_lo * acc_lo, axis=-1, keepdims=True)
          + jnp.sum(acc_hi * acc_hi, axis=-1, keepdims=True)) / (2 * half)
    scale = lax.rsqrt(ms + EPS)
    o_ref[:, :half] = acc_lo * scale * g_ref[:, :half]
    o_ref[:, half:] = acc_hi * scale * g_ref[:, half:]


def _combine(dest, x1, topw_cols, g, y, tm=COMB_TM):
    n, d = x1.shape
    nt = n // tm
    return pl.pallas_call(
        _combine_kernel,
        out_shape=jax.ShapeDtypeStruct((n, d), F32),
        grid=(nt,),
        in_specs=[pl.BlockSpec((1, TOP_K, tm), lambda i: (i, 0, 0), memory_space=pltpu.SMEM),
                  pl.BlockSpec((1, TOP_K, tm), lambda i: (jnp.minimum(i + 1, nt - 1), 0, 0),
                               memory_space=pltpu.SMEM),
                  pl.BlockSpec((tm, d), lambda i: (i, 0)),
                  pl.BlockSpec((tm, TOP_K), lambda i: (i, 0)),
                  pl.BlockSpec((1, d), lambda i: (0, 0)),
                  pl.BlockSpec(memory_space=pl.ANY)],
        out_specs=pl.BlockSpec((tm, d), lambda i: (i, 0)),
        scratch_shapes=[pltpu.VMEM((2, TOP_K, tm, d // 2), jnp.uint32),
                        pltpu.SemaphoreType.DMA((2,))],
        compiler_params=_cparams("arbitrary"),
        name="combine",
    )(dest, dest, x1, topw_cols, g.reshape(1, d), y)


def _routing_tables(counts, topi, rank, n_blocks, n_ftiles):
    n_exp = counts.shape[0]
    nb = (counts + MOE_TM - 1) // MOE_TM
    bend = jnp.cumsum(nb)
    bstart = bend - nb
    nused = bend[-1]
    pstart = bstart * MOE_TM
    dest = jnp.take(pstart, topi, axis=0).astype(I32) + rank
    b = jnp.arange(n_blocks, dtype=I32)
    bc = jnp.minimum(b, nused - 1)
    be = jnp.sum(bend[None, :] <= bc[:, None], axis=1).astype(I32)
    bvalid = b < nused
    bfirst = bvalid & (bc == bstart[be])
    bnext = b + nb[be]
    b_nxt_e = be[jnp.minimum(bnext, n_blocks - 1)]
    bflags = (FLAG_VALID * bvalid.astype(I32) + FLAG_FIRST * bfirst.astype(I32)
              + FLAG_HAS_NEXT * (bfirst & (bnext < nused)).astype(I32))
    q = jnp.arange(n_ftiles * n_blocks, dtype=I32)
    total = n_ftiles * nused
    qc = jnp.minimum(q, total - 1)
    qe = jnp.sum((n_ftiles * bend)[None, :] <= qc[:, None], axis=1).astype(I32)
    local = qc - n_ftiles * bstart[qe]
    nbe = jnp.maximum(nb[qe], 1)
    in_f = local // nbe
    in_rb = bstart[qe] + local % nbe
    qvalid = q < total
    tail = jnp.maximum(q - total, 0)
    out_rb = jnp.where(qvalid, in_rb, nused + tail // n_ftiles)
    out_f = jnp.where(qvalid, in_f, tail % n_ftiles)
    qfirst = qvalid & (local % nbe == 0)
    qnext = q + nbe
    qn = jnp.minimum(qnext, q.shape[0] - 1)
    qflags = (FLAG_VALID * qvalid.astype(I32) + FLAG_FIRST * qfirst.astype(I32)
              + FLAG_HAS_NEXT * (qfirst & (qnext < total)).astype(I32))
    items = tuple(a.astype(I32) for a in (in_rb, qe, in_f, out_rb, out_f, qflags, qe[qn], in_f[qn]))
    zstart = jnp.maximum(bend * MOE_TM - MOE_TM, 0).astype(I32)
    npad = (nb * MOE_TM - counts).astype(I32)
    blocks = tuple(a.astype(I32) for a in (bc, be, bflags, b_nxt_e))
    return dest, zstart, npad, nused.astype(I32).reshape(1), blocks, items


def kernel(x, norm1_g, w_in, b_gates, w_pool_out, pool_scale, sgu_norm_g, w_spatial, b_spatial,
           w_sgu_out, w_out, norm2_g, w_router, b_router, w_gate, b_gate, w_up, b_up, w_down,
           b_down, normf_g):
    bsz, seq, d = x.shape
    n = bsz * seq
    groups, gd, _ = w_pool_out.shape
    pool_w = groups * gd
    sgu_w = w_sgu_out.shape[0]
    n_exp = w_router.shape[1]
    xf = x.reshape(n, d)

    h, pooled = _inproj_pool(xf, norm1_g, w_in[:, :pool_w].astype(BF16), groups, gd, seq)
    s = _inproj_sgu(h, w_in, pool_w, pool_w + sgu_w, sgu_w, sgu_norm_g, w_spatial, b_spatial)
    m = _mix(h, w_in, pool_w + 2 * sgu_w, b_gates, pooled, w_pool_out.astype(BF16), pool_scale,
             s, w_sgu_out.astype(BF16))
    x1 = _outproj(m, w_out, xf)

    h2, topi, rank, topw, counts = _router(x1, norm2_g, w_router, b_router)

    n_blocks = (n * TOP_K) // MOE_TM + n_exp
    n_ftiles = w_gate.shape[2] // MOE_TF
    dest, zstart, npad, nused, blocks, items = _routing_tables(
        counts[:, 0], topi, rank, n_blocks, n_ftiles)

    xs = _dispatch(zstart, npad, nused, dest, h2, n_blocks * MOE_TM)
    hm = _expert_up(items, xs, w_gate, w_up, b_gate, b_up)
    y = _expert_down(*blocks, hm, w_down, b_down)
    assert COMB_TM == TOK_TM
    out = _combine(dest, x1, topw.T, normf_g, y)
    return out.reshape(bsz, seq, d)
```

```python
import functools
import math

import jax
import jax.numpy as jnp
from jax import lax
from jax.experimental import pallas as pl
from jax.experimental.pallas import tpu as pltpu

CHUNK = 64
POOL_WINDOWS = (2, 4, 8, 16)
SGU_HEADS = 8
SGU_BLOCK = 128
TOP_K = 4
SWIGLU_LIMIT = 7.0
SWIGLU_ALPHA = 1.702
EPS = 1e-5

VMEM_LIMIT_BYTES = 58 * 1024 * 1024
POOL_HALO = 16
MOE_TM = 512
MOE_ROW_Q = 128
MOE_TF = 512
TOK_TM = 256
COMB_TM = 256
DMA_UNROLL = 8

BF16 = jnp.bfloat16
F32 = jnp.float32
I32 = jnp.int32


def _cparams(*sem):
    return pltpu.CompilerParams(dimension_semantics=sem, vmem_limit_bytes=VMEM_LIMIT_BYTES)


def _gelu_tanh(x):
    c = math.sqrt(2.0 / math.pi)
    return x * (0.5 * (1.0 + jnp.tanh(c * (x + 0.044715 * (x * x * x)))))


def _rms_scale(x):
    return lax.rsqrt(jnp.mean(x * x, axis=-1, keepdims=True) + EPS)


def _inproj_pool_kernel(x_ref, g_ref, w_ref, h_ref, o_ref, buf_ref, carry_ref, *, seq):
    i = pl.program_id(0)
    tm = x_ref.shape[0]
    gd = buf_ref.shape[-1]
    x = x_ref[...]
    h = (x * _rms_scale(x) * g_ref[...]).astype(h_ref.dtype)
    h_ref[...] = h
    seq_pos0 = (i * tm) % seq
    t1 = seq_pos0 + 1 + lax.broadcasted_iota(I32, (tm, 1), 0)

    for gi, win in enumerate(POOL_WINDOWS):
        cols = slice(gi * gd, (gi + 1) * gd)
        p = jnp.dot(h, w_ref[:, cols], preferred_element_type=F32)
        buf_ref[gi, 0:POOL_HALO, :] = jnp.where(seq_pos0 == 0, 0.0, carry_ref[gi])
        buf_ref[gi, POOL_HALO:, :] = p
        carry_ref[gi] = p[tm - POOL_HALO:, :]
        s = buf_ref[gi]
        shift = 1
        while shift < win:
            s = s + pltpu.roll(s, shift, 0)
            shift *= 2
        denom = jnp.minimum(t1, win).astype(F32)
        o_ref[:, cols] = (s[POOL_HALO:, :] / denom - p).astype(o_ref.dtype)


def _inproj_pool(x, g, w, n_groups, gd, seq, tm=512):
    n, d = x.shape
    assert seq % tm == 0 and n_groups == len(POOL_WINDOWS) and max(POOL_WINDOWS) <= POOL_HALO
    pw = n_groups * gd
    return pl.pallas_call(
        functools.partial(_inproj_pool_kernel, seq=seq),
        out_shape=(jax.ShapeDtypeStruct((n, d), BF16),
                   jax.ShapeDtypeStruct((n, pw), BF16)),
        grid=(n // tm,),
        in_specs=[pl.BlockSpec((tm, d), lambda i: (i, 0)),
                  pl.BlockSpec((1, d), lambda i: (0, 0)),
                  pl.BlockSpec((d, pw), lambda i: (0, 0), pipeline_mode=pl.Buffered(1))],
        out_specs=(pl.BlockSpec((tm, d), lambda i: (i, 0)),
                   pl.BlockSpec((tm, pw), lambda i: (i, 0))),
        scratch_shapes=[pltpu.VMEM((n_groups, POOL_HALO + tm, gd), F32),
                        pltpu.VMEM((n_groups, POOL_HALO, gd), F32)],
        compiler_params=_cparams("arbitrary"),
        name="inproj_pool",
    )(x, g.reshape(1, d), w)


def _stage_weight_tiles(w_hbm, col0s, tn, stages, casts, sems):
    j = pl.program_id(0)
    i = pl.program_id(1)

    def copies(jj):
        return [pltpu.make_async_copy(
                    w_hbm.at[:, pl.ds(pl.multiple_of(c0 + jj * tn, tn), tn)], st, sems.at[k])
                for k, (c0, st) in enumerate(zip(col0s, stages))]

    @pl.when((j == 0) & (i == 0))
    def _():
        for c in copies(0):
            c.start()

    @pl.when(i == 0)
    def _():
        for c in copies(j):
            c.wait()
        for st, dst in zip(stages, casts):
            dst[...] = st[...].astype(dst.dtype)

        @pl.when(j + 1 < pl.num_programs(0))
        def _():
            for c in copies(j + 1):
                c.start()


def _inproj_sgu_kernel(h_ref, g_ref, ws_ref, b_ref, w_hbm, s_ref, stage_u, stage_v, wu_ref, wv_ref,
                       sems, *, head_dim, u_col0, v_col0):
    _stage_weight_tiles(w_hbm, (u_col0, v_col0), s_ref.shape[1], (stage_u, stage_v),
                        (wu_ref, wv_ref), sems)
    h = h_ref[...]
    u = _gelu_tanh(jnp.dot(h, wu_ref[...], preferred_element_type=F32))
    v = _gelu_tanh(jnp.dot(h, wv_ref[...], preferred_element_type=F32))
    pos_i = lax.broadcasted_iota(I32, (SGU_BLOCK, SGU_BLOCK), 0)
    pos_j = lax.broadcasted_iota(I32, (SGU_BLOCK, SGU_BLOCK), 1)
    readable = (pos_j // CHUNK) <= (pos_i // CHUNK)
    for hh in range(u.shape[1] // head_dim):
        cols = slice(hh * head_dim, (hh + 1) * head_dim)
        blk = v[:, cols]
        vn = (blk * _rms_scale(blk) * g_ref[:, cols]).astype(BF16)
        ws = jnp.where(readable, ws_ref[hh], 0.0).astype(BF16)
        bias = b_ref[hh]
        for c in range(u.shape[0] // SGU_BLOCK):
            rows = slice(c * SGU_BLOCK, (c + 1) * SGU_BLOCK)
            mixed = jnp.dot(ws, vn[rows, :], preferred_element_type=F32) + bias
            s_ref[rows, cols] = (u[rows, cols] * mixed).astype(s_ref.dtype)


def _inproj_sgu(h, w, u_col0, v_col0, width, norm_g, w_spatial, b_spatial, tm=1024, tn=512):
    n, d = h.shape
    hd = width // SGU_HEADS
    hpt = tn // hd
    assert u_col0 % tn == 0 and v_col0 % tn == 0
    return pl.pallas_call(
        functools.partial(_inproj_sgu_kernel, head_dim=hd, u_col0=u_col0, v_col0=v_col0),
        out_shape=jax.ShapeDtypeStruct((n, width), BF16),
        grid=(width // tn, n // tm),
        in_specs=[pl.BlockSpec((tm, d), lambda j, i: (i, 0)),
                  pl.BlockSpec((1, tn), lambda j, i: (0, j)),
                  pl.BlockSpec((hpt, SGU_BLOCK, SGU_BLOCK), lambda j, i: (j, 0, 0)),
                  pl.BlockSpec((hpt, SGU_BLOCK, 1), lambda j, i: (j, 0, 0)),
                  pl.BlockSpec(memory_space=pl.ANY)],
        out_specs=pl.BlockSpec((tm, tn), lambda j, i: (i, j)),
        scratch_shapes=[pltpu.VMEM((d, tn), F32), pltpu.VMEM((d, tn), F32),
                        pltpu.VMEM((d, tn), BF16), pltpu.VMEM((d, tn), BF16),
                        pltpu.SemaphoreType.DMA((2,))],
        compiler_params=_cparams("arbitrary", "arbitrary"),
        name="inproj_sgu",
    )(h, norm_g.reshape(1, width), w_spatial, b_spatial.reshape(SGU_HEADS, SGU_BLOCK, 1), w)


def _mix_kernel(h_ref, ba_ref, bb_ref, pooled_ref, wp_ref, scale_ref, s_ref, wsgu_ref, w_hbm,
                m_ref, stage_a, stage_b, wga_ref, wgb_ref, sems, *, ga_col0, gb_col0):
    _stage_weight_tiles(w_hbm, (ga_col0, gb_col0), m_ref.shape[1], (stage_a, stage_b),
                        (wga_ref, wgb_ref), sems)
    h = h_ref[...]
    g_a = jax.nn.sigmoid(jnp.dot(h, wga_ref[...], preferred_element_type=F32) + ba_ref[...])
    y_a = jnp.dot(pooled_ref[...], wp_ref[0], preferred_element_type=F32) * scale_ref[...]
    m = g_a * y_a
    g_b = jax.nn.sigmoid(jnp.dot(h, wgb_ref[...], preferred_element_type=F32) + bb_ref[...])
    y_b = jnp.dot(s_ref[...], wsgu_ref[...], preferred_element_type=F32)
    m_ref[...] = (m + g_b * y_b).astype(m_ref.dtype)


def _mix(h, w_in, gate_col0, b_gates, pooled, w_pool, pool_scale, s, w_sgu, tm=512, tn=512):
    n, d = h.shape
    groups, gd, od = w_pool.shape
    sw = s.shape[1]
    tpg = od // tn
    nd = d // tn
    assert gate_col0 % tn == 0
    return pl.pallas_call(
        functools.partial(_mix_kernel, ga_col0=gate_col0, gb_col0=gate_col0 + d),
        out_shape=jax.ShapeDtypeStruct((n, d), BF16),
        grid=(nd, n // tm),
        in_specs=[pl.BlockSpec((tm, d), lambda j, i: (i, 0)),
                  pl.BlockSpec((1, tn), lambda j, i: (0, j)),
                  pl.BlockSpec((1, tn), lambda j, i: (0, j + nd)),
                  pl.BlockSpec((tm, gd), lambda j, i: (i, j // tpg)),
                  pl.BlockSpec((1, gd, tn), lambda j, i: (j // tpg, 0, j % tpg)),
                  pl.BlockSpec((1, tn), lambda j, i: (0, j)),
                  pl.BlockSpec((tm, sw), lambda j, i: (i, 0)),
                  pl.BlockSpec((sw, tn), lambda j, i: (0, j)),
                  pl.BlockSpec(memory_space=pl.ANY)],
        out_specs=pl.BlockSpec((tm, tn), lambda j, i: (i, j)),
        scratch_shapes=[pltpu.VMEM((d, tn), F32), pltpu.VMEM((d, tn), F32),
                        pltpu.VMEM((d, tn), BF16), pltpu.VMEM((d, tn), BF16),
                        pltpu.SemaphoreType.DMA((2,))],
        compiler_params=_cparams("arbitrary", "arbitrary"),
        name="mix",
    )(h, b_gates.reshape(1, 2 * d), b_gates.reshape(1, 2 * d), pooled, w_pool,
      pool_scale.reshape(1, d), s, w_sgu, w_in)


def _outproj_kernel(m_ref, x_ref, w_hbm, o_ref, stage, w_ref, sems):
    _stage_weight_tiles(w_hbm, (0,), o_ref.shape[1], (stage,), (w_ref,), sems)
    o_ref[...] = x_ref[...] + jnp.dot(m_ref[...], w_ref[...], preferred_element_type=F32)


def _outproj(m, w, x, tm=512, tn=1024):
    n, d = m.shape
    return pl.pallas_call(
        _outproj_kernel,
        out_shape=jax.ShapeDtypeStruct((n, d), F32),
        grid=(d // tn, n // tm),
        in_specs=[pl.BlockSpec((tm, d), lambda j, i: (i, 0)),
                  pl.BlockSpec((tm, tn), lambda j, i: (i, j)),
                  pl.BlockSpec(memory_space=pl.ANY)],
        out_specs=pl.BlockSpec((tm, tn), lambda j, i: (i, j)),
        scratch_shapes=[pltpu.VMEM((d, tn), F32), pltpu.VMEM((d, tn), BF16),
                        pltpu.SemaphoreType.DMA((1,))],
        compiler_params=_cparams("arbitrary", "arbitrary"),
        name="outproj",
    )(m, x, w)


def _pack_bf16_pair(lo, hi):
    lo_bits = lax.bitcast_convert_type(lo.astype(BF16).astype(F32), jnp.uint32)
    hi_bits = lax.bitcast_convert_type(hi.astype(BF16).astype(F32), jnp.uint32)
    return (hi_bits & jnp.uint32(0xFFFF0000)) | (lo_bits >> 16)


def _unpack_bf16_pair(words):
    lo = lax.bitcast_convert_type(words << 16, F32).astype(BF16)
    hi = lax.bitcast_convert_type(words & jnp.uint32(0xFFFF0000), F32).astype(BF16)
    return lo, hi


def _router_kernel(x_ref, g_ref, wr_ref, br_ref, h2_ref, topi_ref, rank_ref, topw_ref, cnt_ref,
                   carry_ref):
    i = pl.program_id(0)
    tm, d = x_ref.shape
    n_exp = wr_ref.shape[0]
    half = d // 2

    @pl.when(i == 0)
    def _():
        carry_ref[...] = jnp.zeros_like(carry_ref)

    x = x_ref[...]
    h2 = x * _rms_scale(x) * g_ref[...]
    h2_ref[...] = _pack_bf16_pair(h2[:, :half], h2[:, half:])

    logits = lax.dot_general(wr_ref[...].astype(BF16), h2.astype(BF16),
                             (((1,), (1,)), ((), ())), preferred_element_type=F32)
    logits = logits + br_ref[...]

    e_iota = lax.broadcasted_iota(I32, (n_exp, tm), 0)
    vals = logits
    top_v, sels = [], []
    for k in range(TOP_K):
        mx = jnp.max(vals, axis=0, keepdims=True)
        idx = jnp.min(jnp.where(vals == mx, e_iota, n_exp), axis=0, keepdims=True)
        sel = e_iota == idx
        vals = jnp.where(sel, -jnp.inf, vals)
        top_v.append(mx)
        sels.append(sel)
        topi_ref[0, k:k + 1, :] = idx

    exps = [jnp.exp(v - top_v[0]) for v in top_v]
    denom = exps[0] + exps[1] + exps[2] + exps[3]
    for k in range(TOP_K):
        topw_ref[k:k + 1, :] = exps[k] / denom

    chosen = jnp.zeros((n_exp, tm), F32)
    for sel in sels:
        chosen = chosen + sel.astype(F32)
    earlier = (lax.broadcasted_iota(I32, (tm, tm), 0)
               < lax.broadcasted_iota(I32, (tm, tm), 1)).astype(BF16)
    before = jnp.dot(chosen.astype(BF16), earlier, preferred_element_type=F32) + carry_ref[...]
    for k in range(TOP_K):
        r = jnp.sum(jnp.where(sels[k], before, 0.0), axis=0, keepdims=True)
        rank_ref[0, k:k + 1, :] = r.astype(I32)
    carry_ref[...] = carry_ref[...] + jnp.sum(chosen, axis=1, keepdims=True)
    cnt_ref[...] = carry_ref[...].astype(I32)


def _router(x1, g, w_router, b_router, tm=TOK_TM):
    n, d = x1.shape
    n_exp = w_router.shape[1]
    nt = n // tm
    return pl.pallas_call(
        _router_kernel,
        out_shape=(jax.ShapeDtypeStruct((n, d // 2), jnp.uint32),
                   jax.ShapeDtypeStruct((nt, TOP_K, tm), I32),
                   jax.ShapeDtypeStruct((nt, TOP_K, tm), I32),
                   jax.ShapeDtypeStruct((TOP_K, n), F32),
                   jax.ShapeDtypeStruct((n_exp, 1), I32)),
        grid=(nt,),
        in_specs=[pl.BlockSpec((tm, d), lambda i: (i, 0)),
                  pl.BlockSpec((1, d), lambda i: (0, 0)),
                  pl.BlockSpec((n_exp, d), lambda i: (0, 0)),
                  pl.BlockSpec((n_exp, 1), lambda i: (0, 0))],
        out_specs=(pl.BlockSpec((tm, d // 2), lambda i: (i, 0)),
                   pl.BlockSpec((1, TOP_K, tm), lambda i: (i, 0, 0)),
                   pl.BlockSpec((1, TOP_K, tm), lambda i: (i, 0, 0)),
                   pl.BlockSpec((TOP_K, tm), lambda i: (0, i)),
                   pl.BlockSpec((n_exp, 1), lambda i: (0, 0))),
        scratch_shapes=[pltpu.VMEM((n_exp, 1), F32)],
        compiler_params=_cparams("arbitrary"),
        name="router",
    )(x1, g.reshape(1, d), w_router.T, b_router.reshape(n_exp, 1))


def _dispatch_kernel(zstart_ref, npad_ref, nused_ref, dest_ref, h2_ref, xs_ref, zero_ref, sem):
    i = pl.program_id(0)
    tm = h2_ref.shape[0]
    n_exp = zstart_ref.shape[0]
    n_blocks = xs_ref.shape[0] // MOE_TM

    @pl.when(i == 0)
    def _():
        zero_ref[...] = jnp.zeros_like(zero_ref)

        def zero_copy(row0):
            return pltpu.make_async_copy(
                zero_ref, xs_ref.at[pl.ds(pl.multiple_of(row0, MOE_TM), MOE_TM)], sem)

        def pad_start(e, c):
            @pl.when(npad_ref[e] > 0)
            def _():
                zero_copy(zstart_ref[e]).start()
            return c

        def pad_wait(e, c):
            @pl.when(npad_ref[e] > 0)
            def _():
                zero_copy(zstart_ref[e]).wait()
            return c

        def tail_start(b, c):
            zero_copy(b * MOE_TM).start()
            return c

        def tail_wait(b, c):
            zero_copy(b * MOE_TM).wait()
            return c

        lax.fori_loop(0, n_exp, pad_start, 0)
        lax.fori_loop(nused_ref[0], n_blocks, tail_start, 0)
        lax.fori_loop(0, n_exp, pad_wait, 0)
        lax.fori_loop(nused_ref[0], n_blocks, tail_wait, 0)

    def start(t, c):
        for k in range(TOP_K):
            pltpu.make_async_copy(h2_ref.at[pl.ds(t, 1)],
                                  xs_ref.at[pl.ds(dest_ref[0, k, t], 1)], sem).start()
        return c

    def wait(t, c):
        for k in range(TOP_K):
            pltpu.make_async_copy(h2_ref.at[pl.ds(0, 1)], xs_ref.at[pl.ds(0, 1)], sem).wait()
        return c

    lax.fori_loop(0, tm, start, 0, unroll=DMA_UNROLL)
    lax.fori_loop(0, tm, wait, 0, unroll=DMA_UNROLL)


def _dispatch(zstart, npad, nused, dest, h2, n_rows, tm=TOK_TM):
    n, hw = h2.shape
    return pl.pallas_call(
        _dispatch_kernel,
        out_shape=jax.ShapeDtypeStruct((n_rows, hw), jnp.uint32),
        grid_spec=pltpu.PrefetchScalarGridSpec(
            num_scalar_prefetch=3,
            grid=(n // tm,),
            in_specs=[pl.BlockSpec((1, TOP_K, tm), lambda i, *_: (i, 0, 0), memory_space=pltpu.SMEM),
                      pl.BlockSpec((tm, hw), lambda i, *_: (i, 0))],
            out_specs=pl.BlockSpec(memory_space=pl.ANY),
            scratch_shapes=[pltpu.VMEM((MOE_TM, hw), jnp.uint32),
                            pltpu.SemaphoreType.DMA(())],
        ),
        compiler_params=_cparams("arbitrary"),
        name="dispatch",
    )(zstart, npad, nused, dest, h2)


FLAG_VALID, FLAG_FIRST, FLAG_HAS_NEXT = 1, 2, 4
FLAG_ROWS_SHIFT = 3


def _for_rows_in_use(flags, body):
    cls = flags >> FLAG_ROWS_SHIFT
    for c in range(1, MOE_TM // MOE_ROW_Q + 1):
        pl.when(cls == c)(functools.partial(body, c * MOE_ROW_Q))


def _expert_up_kernel(in_rb, in_e, in_f, out_rb, out_f, flags, nxt_e, nxt_f,
                      xs_ref, bg_ref, bu_ref, wg_hbm, wu_hbm, hm_ref,
                      stage_g, stage_u, wg_s, wu_s, sems):
    q = pl.program_id(0)
    fl = flags[q]

    def weight_copies(e, f):
        cols = pl.ds(pl.multiple_of(f * MOE_TF, MOE_TF), MOE_TF)
        return (pltpu.make_async_copy(wg_hbm.at[e, :, cols], stage_g, sems.at[0]),
                pltpu.make_async_copy(wu_hbm.at[e, :, cols], stage_u, sems.at[1]))

    @pl.when(q == 0)
    def _():
        for c in weight_copies(in_e[0], in_f[0]):
            c.start()

    @pl.when((fl & FLAG_FIRST) != 0)
    def _():
        for c in weight_copies(in_e[q], in_f[q]):
            c.wait()
        wg_s[...] = stage_g[...].astype(BF16)
        wu_s[...] = stage_u[...].astype(BF16)

        @pl.when((fl & FLAG_HAS_NEXT) != 0)
        def _():
            for c in weight_copies(nxt_e[q], nxt_f[q]):
                c.start()

    def compute(rows):
        half = xs_ref.shape[1]
        lo, hi = _unpack_bf16_pair(xs_ref[:rows, :])

        def proj(w_s, b_ref):
            return (jnp.dot(lo, w_s[:half, :], preferred_element_type=F32)
                    + jnp.dot(hi, w_s[half:, :], preferred_element_type=F32) + b_ref[0])

        gate = jnp.minimum(proj(wg_s, bg_ref), SWIGLU_LIMIT)
        up = jnp.clip(proj(wu_s, bu_ref), -SWIGLU_LIMIT, SWIGLU_LIMIT)
        glu = gate * jax.nn.sigmoid(SWIGLU_ALPHA * gate)
        hm_ref[:rows, :] = ((up + 1.0) * glu).astype(hm_ref.dtype)
        if rows < MOE_TM:
            hm_ref[rows:, :] = jnp.zeros((MOE_TM - rows, hm_ref.shape[1]), hm_ref.dtype)

    _for_rows_in_use(fl, compute)

    @pl.when((fl & FLAG_VALID) == 0)
    def _():
        hm_ref[...] = jnp.zeros_like(hm_ref)


def _expert_up(items, xs, wg, wu, bg, bu):
    n_rows, hw = xs.shape
    n_exp, d, f = wg.shape
    n_items = items[0].shape[0]
    b_spec = pl.BlockSpec((1, 1, MOE_TF), lambda q, rb, e, ft, *_: (e[q], 0, ft[q]))
    return pl.pallas_call(
        _expert_up_kernel,
        out_shape=jax.ShapeDtypeStruct((n_rows, f), BF16),
        grid_spec=pltpu.PrefetchScalarGridSpec(
            num_scalar_prefetch=len(items),
            grid=(n_items,),
            in_specs=[pl.BlockSpec((MOE_TM, hw), lambda q, rb, *_: (rb[q], 0)),
                      b_spec, b_spec,
                      pl.BlockSpec(memory_space=pl.ANY),
                      pl.BlockSpec(memory_space=pl.ANY)],
            out_specs=pl.BlockSpec((MOE_TM, MOE_TF),
                                   lambda q, rb, e, ft, orb, oft, *_: (orb[q], oft[q])),
            scratch_shapes=[pltpu.VMEM((d, MOE_TF), F32), pltpu.VMEM((d, MOE_TF), F32),
                            pltpu.VMEM((d, MOE_TF), BF16), pltpu.VMEM((d, MOE_TF), BF16),
                            pltpu.SemaphoreType.DMA((2,))],
        ),
        compiler_params=_cparams("arbitrary"),
        name="expert_up",
    )(*items, xs, bg.reshape(n_exp, 1, f), bu.reshape(n_exp, 1, f), wg, wu)


def _expert_down_kernel(bidx_ref, be_ref, flags, nxt_e, hm_ref, bd_ref, wd_hbm, y_ref,
                        stage, wd_s, sem):
    b = pl.program_id(0)
    fl = flags[b]
    half = y_ref.shape[1]

    def weight_copy(e):
        return pltpu.make_async_copy(wd_hbm.at[e], stage, sem)

    @pl.when(b == 0)
    def _():
        weight_copy(be_ref[0]).start()

    @pl.when((fl & FLAG_FIRST) != 0)
    def _():
        weight_copy(be_ref[b]).wait()
        wd_s[...] = stage[...].astype(BF16)

        @pl.when((fl & FLAG_HAS_NEXT) != 0)
        def _():
            weight_copy(nxt_e[b]).start()

    def compute(rows):
        y = jnp.dot(hm_ref[:rows, :], wd_s[...], preferred_element_type=F32) + bd_ref[0]
        y_ref[:rows, :] = _pack_bf16_pair(y[:, :half], y[:, half:])
        if rows < MOE_TM:
            y_ref[rows:, :] = jnp.zeros((MOE_TM - rows, half), y_ref.dtype)

    _for_rows_in_use(fl, compute)

    @pl.when((fl & FLAG_VALID) == 0)
    def _():
        y_ref[...] = jnp.zeros_like(y_ref)


def _expert_down(bidx, be, flags, nxt_e, hm, wd, bd):
    n_rows, f = hm.shape
    n_exp, _, d = wd.shape
    return pl.pallas_call(
        _expert_down_kernel,
        out_shape=jax.ShapeDtypeStruct((n_rows, d // 2), jnp.uint32),
        grid_spec=pltpu.PrefetchScalarGridSpec(
            num_scalar_prefetch=4,
            grid=(n_rows // MOE_TM,),
            in_specs=[pl.BlockSpec((MOE_TM, f), lambda b, bi, *_: (bi[b], 0)),
                      pl.BlockSpec((1, 1, d), lambda b, bi, be, *_: (be[b], 0, 0)),
                      pl.BlockSpec(memory_space=pl.ANY)],
            out_specs=pl.BlockSpec((MOE_TM, d // 2), lambda b, *_: (b, 0)),
            scratch_shapes=[pltpu.VMEM((f, d), F32), pltpu.VMEM((f, d), BF16),
                            pltpu.SemaphoreType.DMA(())],
        ),
        compiler_params=_cparams("arbitrary"),
        name="expert_down",
    )(bidx, be, flags, nxt_e, hm, bd.reshape(n_exp, 1, d), wd)


def _combine_kernel(dest_ref, dest_next_ref, x1_ref, w_ref, g_ref, y_ref, o_ref, buf_ref, sems):
    i = pl.program_id(0)
    nt = pl.num_programs(0)
    tm = x1_ref.shape[0]
    slot = i % 2

    def issue(d_ref, sl):
        def start(t, c):
            for k in range(TOP_K):
                pltpu.make_async_copy(y_ref.at[pl.ds(d_ref[0, k, t], 1)],
                                      buf_ref.at[sl, k, pl.ds(t, 1)], sems.at[sl]).start()
            return c
        lax.fori_loop(0, tm, start, 0, unroll=DMA_UNROLL)

    @pl.when(i == 0)
    def _():
        issue(dest_ref, 0)

    @pl.when(i + 1 < nt)
    def _():
        issue(dest_next_ref, 1 - slot)

    def wait(t, c):
        for k in range(TOP_K):
            pltpu.make_async_copy(y_ref.at[pl.ds(0, 1)], buf_ref.at[slot, 0, pl.ds(0, 1)],
                                  sems.at[slot]).wait()
        return c

    lax.fori_loop(0, tm, wait, 0, unroll=DMA_UNROLL)

    half = buf_ref.shape[-1]
    acc_lo = x1_ref[:, :half]
    acc_hi = x1_ref[:, half:]
    for k in range(TOP_K):
        words = buf_ref[slot, k]
        wk = w_ref[:, k:k + 1]
        acc_lo = acc_lo + wk * lax.bitcast_convert_type(words << 16, F32)
        acc_hi = acc_hi + wk * lax.bitcast_convert_type(words & jnp.uint32(0xFFFF0000), F32)
    ms = (jnp.sum(acc_lo * acc_lo, axis=-1, keepdims=True)
          + jnp.sum(acc_hi * acc_hi, axis=-1, keepdims=True)) / (2 * half)
    scale = lax.rsqrt(ms + EPS)
    o_ref[:, :half] = acc_lo * scale * g_ref[:, :half]
    o_ref[:, half:] = acc_hi * scale * g_ref[:, half:]


def _combine(dest, x1, topw_cols, g, y, tm=COMB_TM):
    n, d = x1.shape
    nt = n // tm
    return pl.pallas_call(
        _combine_kernel,
        out_shape=jax.ShapeDtypeStruct((n, d), F32),
        grid=(nt,),
        in_specs=[pl.BlockSpec((1, TOP_K, tm), lambda i: (i, 0, 0), memory_space=pltpu.SMEM),
                  pl.BlockSpec((1, TOP_K, tm), lambda i: (jnp.minimum(i + 1, nt - 1), 0, 0),
                               memory_space=pltpu.SMEM),
                  pl.BlockSpec((tm, d), lambda i: (i, 0)),
                  pl.BlockSpec((tm, TOP_K), lambda i: (i, 0)),
                  pl.BlockSpec((1, d), lambda i: (0, 0)),
                  pl.BlockSpec(memory_space=pl.ANY)],
        out_specs=pl.BlockSpec((tm, d), lambda i: (i, 0)),
        scratch_shapes=[pltpu.VMEM((2, TOP_K, tm, d // 2), jnp.uint32),
                        pltpu.SemaphoreType.DMA((2,))],
        compiler_params=_cparams("arbitrary"),
        name="combine",
    )(dest, dest, x1, topw_cols, g.reshape(1, d), y)


def _routing_tables(counts, topi, rank, n_blocks, n_ftiles):
    def lookup(table, idx):
        hit = idx[..., None] == jnp.arange(table.shape[0], dtype=I32)
        return jnp.sum(jnp.where(hit, table, 0), axis=-1).astype(I32)

    nb = (counts + MOE_TM - 1) // MOE_TM
    bend = jnp.cumsum(nb)
    bstart = bend - nb
    nused = bend[-1]
    pstart = bstart * MOE_TM
    dest = lookup(pstart, topi) + rank
    b = jnp.arange(n_blocks, dtype=I32)
    bc = jnp.minimum(b, nused - 1)
    be = jnp.sum(bend[None, :] <= bc[:, None], axis=1).astype(I32)
    bvalid = b < nused
    bstart_b = lookup(bstart, be)
    bfirst = bvalid & (bc == bstart_b)
    bnext = b + lookup(nb, be)
    b_nxt_e = lookup(be, jnp.minimum(bnext, n_blocks - 1))
    rows_used = jnp.clip(lookup(counts, be) - (bc - bstart_b) * MOE_TM, 1, MOE_TM)
    bcls = jnp.where(bvalid, (rows_used + MOE_ROW_Q - 1) // MOE_ROW_Q, 0)
    bflags = (FLAG_VALID * bvalid.astype(I32) + FLAG_FIRST * bfirst.astype(I32)
              + FLAG_HAS_NEXT * (bfirst & (bnext < nused)).astype(I32) + (bcls << FLAG_ROWS_SHIFT))
    q = jnp.arange(n_ftiles * n_blocks, dtype=I32)
    total = n_ftiles * nused
    qc = jnp.minimum(q, total - 1)
    qe = jnp.sum((n_ftiles * bend)[None, :] <= qc[:, None], axis=1).astype(I32)
    bstart_q = lookup(bstart, qe)
    local = qc - n_ftiles * bstart_q
    nbe = jnp.maximum(lookup(nb, qe), 1)
    in_f = local // nbe
    in_rb = bstart_q + local % nbe
    qvalid = q < total
    tail = jnp.maximum(q - total, 0)
    out_rb = jnp.where(qvalid, in_rb, nused + tail // n_ftiles)
    out_f = jnp.where(qvalid, in_f, tail % n_ftiles)
    qfirst = qvalid & (local % nbe == 0)
    qnext = q + nbe
    qn = jnp.minimum(qnext, q.shape[0] - 1)
    qcls = jnp.where(qvalid, lookup(bcls, in_rb), 0)
    qflags = (FLAG_VALID * qvalid.astype(I32) + FLAG_FIRST * qfirst.astype(I32)
              + FLAG_HAS_NEXT * (qfirst & (qnext < total)).astype(I32) + (qcls << FLAG_ROWS_SHIFT))
    items = tuple(a.astype(I32) for a in (in_rb, qe, in_f, out_rb, out_f, qflags,
                                          lookup(qe, qn), lookup(in_f, qn)))
    zstart = jnp.maximum(bend * MOE_TM - MOE_TM, 0).astype(I32)
    npad = (nb * MOE_TM - counts).astype(I32)
    blocks = tuple(a.astype(I32) for a in (bc, be, bflags, b_nxt_e))
    return dest, zstart, npad, nused.astype(I32).reshape(1), blocks, items


def kernel(x, norm1_g, w_in, b_gates, w_pool_out, pool_scale, sgu_norm_g, w_spatial, b_spatial,
           w_sgu_out, w_out, norm2_g, w_router, b_router, w_gate, b_gate, w_up, b_up, w_down,
           b_down, normf_g):
    bsz, seq, d = x.shape
    n = bsz * seq
    groups, gd, _ = w_pool_out.shape
    pool_w = groups * gd
    sgu_w = w_sgu_out.shape[0]
    n_exp = w_router.shape[1]
    xf = x.reshape(n, d)

    h, pooled = _inproj_pool(xf, norm1_g, w_in[:, :pool_w].astype(BF16), groups, gd, seq)
    s = _inproj_sgu(h, w_in, pool_w, pool_w + sgu_w, sgu_w, sgu_norm_g, w_spatial, b_spatial)
    m = _mix(h, w_in, pool_w + 2 * sgu_w, b_gates, pooled, w_pool_out.astype(BF16), pool_scale,
             s, w_sgu_out.astype(BF16))
    x1 = _outproj(m, w_out, xf)

    h2, topi, rank, topw, counts = _router(x1, norm2_g, w_router, b_router)

    n_blocks = (n * TOP_K) // MOE_TM + n_exp
    n_ftiles = w_gate.shape[2] // MOE_TF
    dest, zstart, npad, nused, blocks, items = _routing_tables(
        counts[:, 0], topi, rank, n_blocks, n_ftiles)

    xs = _dispatch(zstart, npad, nused, dest, h2, n_blocks * MOE_TM)
    hm = _expert_up(items, xs, w_gate, w_up, b_gate, b_up)
    y = _expert_down(*blocks, hm, w_down, b_down)
    assert COMB_TM == TOK_TM
    out = _combine(dest, x1, topw.T, normf_g, y)
    return out.reshape(bsz, seq, d)
```

```python
import functools
import math

import jax
import jax.numpy as jnp
from jax import lax
from jax.experimental import pallas as pl
from jax.experimental.pallas import tpu as pltpu

CHUNK = 64
POOL_WINDOWS = (2, 4, 8, 16)
SGU_HEADS = 8
SGU_BLOCK = 128
TOP_K = 4
SWIGLU_LIMIT = 7.0
SWIGLU_ALPHA = 1.702
EPS = 1e-5

VMEM_LIMIT_BYTES = 58 * 1024 * 1024
POOL_HALO = 16
MOE_TM = 512
MOE_ROW_Q = 128
MOE_TF = 512
TOK_TM = 256
COMB_TM = 256
DMA_UNROLL = 8

BF16 = jnp.bfloat16
F32 = jnp.float32
I32 = jnp.int32


def _cparams(*sem):
    return pltpu.CompilerParams(dimension_semantics=sem, vmem_limit_bytes=VMEM_LIMIT_BYTES)


def _gelu_tanh(x):
    c = math.sqrt(2.0 / math.pi)
    return x * (0.5 * (1.0 + jnp.tanh(c * (x + 0.044715 * (x * x * x)))))


def _rms_scale(x):
    return lax.rsqrt(jnp.mean(x * x, axis=-1, keepdims=True) + EPS)


def _inproj_pool_kernel(x_ref, g_ref, w_ref, h_ref, o_ref, buf_ref, carry_ref, *, seq):
    i = pl.program_id(0)
    tm = x_ref.shape[0]
    gd = buf_ref.shape[-1]
    x = x_ref[...]
    h = (x * _rms_scale(x) * g_ref[...]).astype(h_ref.dtype)
    h_ref[...] = h
    seq_pos0 = (i * tm) % seq
    t1 = seq_pos0 + 1 + lax.broadcasted_iota(I32, (tm, 1), 0)

    for gi, win in enumerate(POOL_WINDOWS):
        cols = slice(gi * gd, (gi + 1) * gd)
        p = jnp.dot(h, w_ref[:, cols], preferred_element_type=F32)
        buf_ref[gi, 0:POOL_HALO, :] = jnp.where(seq_pos0 == 0, 0.0, carry_ref[gi])
        buf_ref[gi, POOL_HALO:, :] = p
        carry_ref[gi] = p[tm - POOL_HALO:, :]
        s = buf_ref[gi]
        shift = 1
        while shift < win:
            s = s + pltpu.roll(s, shift, 0)
            shift *= 2
        denom = jnp.minimum(t1, win).astype(F32)
        o_ref[:, cols] = (s[POOL_HALO:, :] / denom - p).astype(o_ref.dtype)


def _inproj_pool(x, g, w, n_groups, gd, seq, tm=512):
    n, d = x.shape
    assert seq % tm == 0 and n_groups == len(POOL_WINDOWS) and max(POOL_WINDOWS) <= POOL_HALO
    pw = n_groups * gd
    return pl.pallas_call(
        functools.partial(_inproj_pool_kernel, seq=seq),
        out_shape=(jax.ShapeDtypeStruct((n, d), BF16),
                   jax.ShapeDtypeStruct((n, pw), BF16)),
        grid=(n // tm,),
        in_specs=[pl.BlockSpec((tm, d), lambda i: (i, 0)),
                  pl.BlockSpec((1, d), lambda i: (0, 0)),
                  pl.BlockSpec((d, pw), lambda i: (0, 0), pipeline_mode=pl.Buffered(1))],
        out_specs=(pl.BlockSpec((tm, d), lambda i: (i, 0)),
                   pl.BlockSpec((tm, pw), lambda i: (i, 0))),
        scratch_shapes=[pltpu.VMEM((n_groups, POOL_HALO + tm, gd), F32),
                        pltpu.VMEM((n_groups, POOL_HALO, gd), F32)],
        compiler_params=_cparams("arbitrary"),
        name="inproj_pool",
    )(x, g.reshape(1, d), w)


def _stage_weight_tiles(w_hbm, col0s, tn, stages, casts, sems):
    j = pl.program_id(0)
    i = pl.program_id(1)

    def copies(jj):
        return [pltpu.make_async_copy(
                    w_hbm.at[:, pl.ds(pl.multiple_of(c0 + jj * tn, tn), tn)], st, sems.at[k])
                for k, (c0, st) in enumerate(zip(col0s, stages))]

    @pl.when((j == 0) & (i == 0))
    def _():
        for c in copies(0):
            c.start()

    @pl.when(i == 0)
    def _():
        for c in copies(j):
            c.wait()
        for st, dst in zip(stages, casts):
            dst[...] = st[...].astype(dst.dtype)

        @pl.when(j + 1 < pl.num_programs(0))
        def _():
            for c in copies(j + 1):
                c.start()


def _inproj_sgu_kernel(h_ref, g_ref, ws_ref, b_ref, w_hbm, s_ref, stage_u, stage_v, wu_ref, wv_ref,
                       sems, *, head_dim, u_col0, v_col0):
    _stage_weight_tiles(w_hbm, (u_col0, v_col0), s_ref.shape[1], (stage_u, stage_v),
                        (wu_ref, wv_ref), sems)
    h = h_ref[...]
    u = _gelu_tanh(jnp.dot(h, wu_ref[...], preferred_element_type=F32))
    v = _gelu_tanh(jnp.dot(h, wv_ref[...], preferred_element_type=F32))
    pos_i = lax.broadcasted_iota(I32, (SGU_BLOCK, SGU_BLOCK), 0)
    pos_j = lax.broadcasted_iota(I32, (SGU_BLOCK, SGU_BLOCK), 1)
    readable = (pos_j // CHUNK) <= (pos_i // CHUNK)
    for hh in range(u.shape[1] // head_dim):
        cols = slice(hh * head_dim, (hh + 1) * head_dim)
        blk = v[:, cols]
        vn = (blk * _rms_scale(blk) * g_ref[:, cols]).astype(BF16)
        ws = jnp.where(readable, ws_ref[hh], 0.0).astype(BF16)
        bias = b_ref[hh]
        for c in range(u.shape[0] // SGU_BLOCK):
            rows = slice(c * SGU_BLOCK, (c + 1) * SGU_BLOCK)
            mixed = jnp.dot(ws, vn[rows, :], preferred_element_type=F32) + bias
            s_ref[rows, cols] = (u[rows, cols] * mixed).astype(s_ref.dtype)


def _inproj_sgu(h, w, u_col0, v_col0, width, norm_g, w_spatial, b_spatial, tm=1024, tn=512):
    n, d = h.shape
    hd = width // SGU_HEADS
    hpt = tn // hd
    assert u_col0 % tn == 0 and v_col0 % tn == 0
    return pl.pallas_call(
        functools.partial(_inproj_sgu_kernel, head_dim=hd, u_col0=u_col0, v_col0=v_col0),
        out_shape=jax.ShapeDtypeStruct((n, width), BF16),
        grid=(width // tn, n // tm),
        in_specs=[pl.BlockSpec((tm, d), lambda j, i: (i, 0)),
                  pl.BlockSpec((1, tn), lambda j, i: (0, j)),
                  pl.BlockSpec((hpt, SGU_BLOCK, SGU_BLOCK), lambda j, i: (j, 0, 0)),
                  pl.BlockSpec((hpt, SGU_BLOCK, 1), lambda j, i: (j, 0, 0)),
                  pl.BlockSpec(memory_space=pl.ANY)],
        out_specs=pl.BlockSpec((tm, tn), lambda j, i: (i, j)),
        scratch_shapes=[pltpu.VMEM((d, tn), F32), pltpu.VMEM((d, tn), F32),
                        pltpu.VMEM((d, tn), BF16), pltpu.VMEM((d, tn), BF16),
                        pltpu.SemaphoreType.DMA((2,))],
        compiler_params=_cparams("arbitrary", "arbitrary"),
        name="inproj_sgu",
    )(h, norm_g.reshape(1, width), w_spatial, b_spatial.reshape(SGU_HEADS, SGU_BLOCK, 1), w)


def _mix_kernel(h_ref, ba_ref, bb_ref, pooled_ref, wp_ref, scale_ref, s_ref, w_hbm, wsgu_hbm,
                m_ref, stage_a, stage_b, stage_s, wga_ref, wgb_ref, wsgu_ref, sems, sems_s,
                *, ga_col0, gb_col0):
    tn = m_ref.shape[1]
    _stage_weight_tiles(w_hbm, (ga_col0, gb_col0), tn, (stage_a, stage_b), (wga_ref, wgb_ref), sems)
    _stage_weight_tiles(wsgu_hbm, (0,), tn, (stage_s,), (wsgu_ref,), sems_s)
    h = h_ref[...]
    g_a = jax.nn.sigmoid(jnp.dot(h, wga_ref[...], preferred_element_type=F32) + ba_ref[...])
    y_a = jnp.dot(pooled_ref[...], wp_ref[0], preferred_element_type=F32) * scale_ref[...]
    m = g_a * y_a
    g_b = jax.nn.sigmoid(jnp.dot(h, wgb_ref[...], preferred_element_type=F32) + bb_ref[...])
    y_b = jnp.dot(s_ref[...], wsgu_ref[...], preferred_element_type=F32)
    m_ref[...] = (m + g_b * y_b).astype(m_ref.dtype)


def _mix(h, w_in, gate_col0, b_gates, pooled, w_pool, pool_scale, s, w_sgu, tm=512, tn=512):
    n, d = h.shape
    groups, gd, od = w_pool.shape
    sw = s.shape[1]
    tpg = od // tn
    nd = d // tn
    assert gate_col0 % tn == 0
    return pl.pallas_call(
        functools.partial(_mix_kernel, ga_col0=gate_col0, gb_col0=gate_col0 + d),
        out_shape=jax.ShapeDtypeStruct((n, d), BF16),
        grid=(nd, n // tm),
        in_specs=[pl.BlockSpec((tm, d), lambda j, i: (i, 0)),
                  pl.BlockSpec((1, tn), lambda j, i: (0, j)),
                  pl.BlockSpec((1, tn), lambda j, i: (0, j + nd)),
                  pl.BlockSpec((tm, gd), lambda j, i: (i, j // tpg)),
                  pl.BlockSpec((1, gd, tn), lambda j, i: (j // tpg, 0, j % tpg)),
                  pl.BlockSpec((1, tn), lambda j, i: (0, j)),
                  pl.BlockSpec((tm, sw), lambda j, i: (i, 0)),
                  pl.BlockSpec(memory_space=pl.ANY),
                  pl.BlockSpec(memory_space=pl.ANY)],
        out_specs=pl.BlockSpec((tm, tn), lambda j, i: (i, j)),
        scratch_shapes=[pltpu.VMEM((d, tn), F32), pltpu.VMEM((d, tn), F32), pltpu.VMEM((sw, tn), F32),
                        pltpu.VMEM((d, tn), BF16), pltpu.VMEM((d, tn), BF16),
                        pltpu.VMEM((sw, tn), BF16),
                        pltpu.SemaphoreType.DMA((2,)), pltpu.SemaphoreType.DMA((1,))],
        compiler_params=_cparams("arbitrary", "arbitrary"),
        name="mix",
    )(h, b_gates.reshape(1, 2 * d), b_gates.reshape(1, 2 * d), pooled, w_pool,
      pool_scale.reshape(1, d), s, w_in, w_sgu)


def _outproj_kernel(m_ref, x_ref, w_hbm, o_ref, stage, w_ref, sems):
    _stage_weight_tiles(w_hbm, (0,), o_ref.shape[1], (stage,), (w_ref,), sems)
    o_ref[...] = x_ref[...] + jnp.dot(m_ref[...], w_ref[...], preferred_element_type=F32)


def _outproj(m, w, x, tm=512, tn=1024):
    n, d = m.shape
    return pl.pallas_call(
        _outproj_kernel,
        out_shape=jax.ShapeDtypeStruct((n, d), F32),
        grid=(d // tn, n // tm),
        in_specs=[pl.BlockSpec((tm, d), lambda j, i: (i, 0)),
                  pl.BlockSpec((tm, tn), lambda j, i: (i, j)),
                  pl.BlockSpec(memory_space=pl.ANY)],
        out_specs=pl.BlockSpec((tm, tn), lambda j, i: (i, j)),
        scratch_shapes=[pltpu.VMEM((d, tn), F32), pltpu.VMEM((d, tn), BF16),
                        pltpu.SemaphoreType.DMA((1,))],
        compiler_params=_cparams("arbitrary", "arbitrary"),
        name="outproj",
    )(m, x, w)


def _pack_bf16_pair(lo, hi):
    lo_bits = lax.bitcast_convert_type(lo.astype(BF16).astype(F32), jnp.uint32)
    hi_bits = lax.bitcast_convert_type(hi.astype(BF16).astype(F32), jnp.uint32)
    return (hi_bits & jnp.uint32(0xFFFF0000)) | (lo_bits >> 16)


def _unpack_bf16_pair(words):
    lo = lax.bitcast_convert_type(words << 16, F32).astype(BF16)
    hi = lax.bitcast_convert_type(words & jnp.uint32(0xFFFF0000), F32).astype(BF16)
    return lo, hi


def _router_kernel(x_ref, g_ref, wr_ref, br_ref, h2_ref, topi_ref, rank_ref, topw_ref, cnt_ref,
                   carry_ref):
    i = pl.program_id(0)
    tm, d = x_ref.shape
    n_exp = wr_ref.shape[0]
    half = d // 2

    @pl.when(i == 0)
    def _():
        carry_ref[...] = jnp.zeros_like(carry_ref)

    x = x_ref[...]
    h2 = x * _rms_scale(x) * g_ref[...]
    h2_ref[...] = _pack_bf16_pair(h2[:, :half], h2[:, half:])

    logits = lax.dot_general(wr_ref[...].astype(BF16), h2.astype(BF16),
                             (((1,), (1,)), ((), ())), preferred_element_type=F32)
    logits = logits + br_ref[...]

    e_iota = lax.broadcasted_iota(I32, (n_exp, tm), 0)
    vals = logits
    top_v, sels = [], []
    for k in range(TOP_K):
        mx = jnp.max(vals, axis=0, keepdims=True)
        idx = jnp.min(jnp.where(vals == mx, e_iota, n_exp), axis=0, keepdims=True)
        sel = e_iota == idx
        vals = jnp.where(sel, -jnp.inf, vals)
        top_v.append(mx)
        sels.append(sel)
        topi_ref[0, k:k + 1, :] = idx

    exps = [jnp.exp(v - top_v[0]) for v in top_v]
    denom = exps[0] + exps[1] + exps[2] + exps[3]
    for k in range(TOP_K):
        topw_ref[k:k + 1, :] = exps[k] / denom

    chosen = jnp.zeros((n_exp, tm), F32)
    for sel in sels:
        chosen = chosen + sel.astype(F32)
    earlier = (lax.broadcasted_iota(I32, (tm, tm), 0)
               < lax.broadcasted_iota(I32, (tm, tm), 1)).astype(BF16)
    before = jnp.dot(chosen.astype(BF16), earlier, preferred_element_type=F32) + carry_ref[...]
    for k in range(TOP_K):
        r = jnp.sum(jnp.where(sels[k], before, 0.0), axis=0, keepdims=True)
        rank_ref[0, k:k + 1, :] = r.astype(I32)
    carry_ref[...] = carry_ref[...] + jnp.sum(chosen, axis=1, keepdims=True)
    cnt_ref[...] = carry_ref[...].astype(I32)


def _router(x1, g, w_router, b_router, tm=TOK_TM):
    n, d = x1.shape
    n_exp = w_router.shape[1]
    nt = n // tm
    return pl.pallas_call(
        _router_kernel,
        out_shape=(jax.ShapeDtypeStruct((n, d // 2), jnp.uint32),
                   jax.ShapeDtypeStruct((nt, TOP_K, tm), I32),
                   jax.ShapeDtypeStruct((nt, TOP_K, tm), I32),
                   jax.ShapeDtypeStruct((TOP_K, n), F32),
                   jax.ShapeDtypeStruct((n_exp, 1), I32)),
        grid=(nt,),
        in_specs=[pl.BlockSpec((tm, d), lambda i: (i, 0)),
                  pl.BlockSpec((1, d), lambda i: (0, 0)),
                  pl.BlockSpec((n_exp, d), lambda i: (0, 0)),
                  pl.BlockSpec((n_exp, 1), lambda i: (0, 0))],
        out_specs=(pl.BlockSpec((tm, d // 2), lambda i: (i, 0)),
                   pl.BlockSpec((1, TOP_K, tm), lambda i: (i, 0, 0)),
                   pl.BlockSpec((1, TOP_K, tm), lambda i: (i, 0, 0)),
                   pl.BlockSpec((TOP_K, tm), lambda i: (0, i)),
                   pl.BlockSpec((n_exp, 1), lambda i: (0, 0))),
        scratch_shapes=[pltpu.VMEM((n_exp, 1), F32)],
        compiler_params=_cparams("arbitrary"),
        name="router",
    )(x1, g.reshape(1, d), w_router.T, b_router.reshape(n_exp, 1))


def _dispatch_kernel(zstart_ref, npad_ref, nused_ref, dest_ref, h2_ref, xs_ref, zero_ref, sem):
    i = pl.program_id(0)
    tm = h2_ref.shape[0]
    n_exp = zstart_ref.shape[0]
    qpb = MOE_TM // MOE_ROW_Q
    n_chunks = xs_ref.shape[0] // MOE_ROW_Q

    @pl.when(i == 0)
    def _():
        zero_ref[...] = jnp.zeros_like(zero_ref)

        def zero_copy(chunk):
            row0 = pl.multiple_of(chunk * MOE_ROW_Q, MOE_ROW_Q)
            return pltpu.make_async_copy(zero_ref, xs_ref.at[pl.ds(row0, MOE_ROW_Q)], sem)

        def for_pad_chunks(fn):
            def per_expert(e, c):
                lax.fori_loop(zstart_ref[e], zstart_ref[e] + npad_ref[e],
                              lambda ch, cc: (fn(ch), cc)[1], 0)
                return c
            lax.fori_loop(0, n_exp, per_expert, 0)

        def for_tail_chunks(fn):
            lax.fori_loop(nused_ref[0] * qpb, n_chunks, lambda ch, cc: (fn(ch), cc)[1], 0)

        for_pad_chunks(lambda ch: zero_copy(ch).start())
        for_tail_chunks(lambda ch: zero_copy(ch).start())
        for_pad_chunks(lambda ch: zero_copy(ch).wait())
        for_tail_chunks(lambda ch: zero_copy(ch).wait())

    def start(t, c):
        for k in range(TOP_K):
            pltpu.make_async_copy(h2_ref.at[pl.ds(t, 1)],
                                  xs_ref.at[pl.ds(dest_ref[0, k, t], 1)], sem).start()
        return c

    def wait(t, c):
        for k in range(TOP_K):
            pltpu.make_async_copy(h2_ref.at[pl.ds(0, 1)], xs_ref.at[pl.ds(0, 1)], sem).wait()
        return c

    lax.fori_loop(0, tm, start, 0, unroll=DMA_UNROLL)
    lax.fori_loop(0, tm, wait, 0, unroll=DMA_UNROLL)


def _dispatch(zstart, npad, nused, dest, h2, n_rows, tm=TOK_TM):
    n, hw = h2.shape
    return pl.pallas_call(
        _dispatch_kernel,
        out_shape=jax.ShapeDtypeStruct((n_rows, hw), jnp.uint32),
        grid_spec=pltpu.PrefetchScalarGridSpec(
            num_scalar_prefetch=3,
            grid=(n // tm,),
            in_specs=[pl.BlockSpec((1, TOP_K, tm), lambda i, *_: (i, 0, 0), memory_space=pltpu.SMEM),
                      pl.BlockSpec((tm, hw), lambda i, *_: (i, 0))],
            out_specs=pl.BlockSpec(memory_space=pl.ANY),
            scratch_shapes=[pltpu.VMEM((MOE_ROW_Q, hw), jnp.uint32),
                            pltpu.SemaphoreType.DMA(())],
        ),
        compiler_params=_cparams("arbitrary"),
        name="dispatch",
    )(zstart, npad, nused, dest, h2)


FLAG_VALID, FLAG_FIRST, FLAG_HAS_NEXT = 1, 2, 4
FLAG_ROWS_SHIFT = 3


def _for_rows_in_use(flags, body):
    cls = flags >> FLAG_ROWS_SHIFT
    for c in range(1, MOE_TM // MOE_ROW_Q + 1):
        pl.when(cls == c)(functools.partial(body, c * MOE_ROW_Q))


def _expert_up_kernel(in_rb, in_e, in_f, out_rb, out_f, flags, nxt_e, nxt_f,
                      xs_ref, bg_ref, bu_ref, wg_hbm, wu_hbm, hm_ref,
                      stage_g, stage_u, wg_s, wu_s, sems):
    q = pl.program_id(0)
    fl = flags[q]

    def weight_copies(e, f):
        cols = pl.ds(pl.multiple_of(f * MOE_TF, MOE_TF), MOE_TF)
        return (pltpu.make_async_copy(wg_hbm.at[e, :, cols], stage_g, sems.at[0]),
                pltpu.make_async_copy(wu_hbm.at[e, :, cols], stage_u, sems.at[1]))

    @pl.when(q == 0)
    def _():
        for c in weight_copies(in_e[0], in_f[0]):
            c.start()

    @pl.when((fl & FLAG_FIRST) != 0)
    def _():
        for c in weight_copies(in_e[q], in_f[q]):
            c.wait()
        wg_s[...] = stage_g[...].astype(BF16)
        wu_s[...] = stage_u[...].astype(BF16)

        @pl.when((fl & FLAG_HAS_NEXT) != 0)
        def _():
            for c in weight_copies(nxt_e[q], nxt_f[q]):
                c.start()

    def compute(rows):
        half = xs_ref.shape[1]
        lo, hi = _unpack_bf16_pair(xs_ref[:rows, :])

        def proj(w_s, b_ref):
            return (jnp.dot(lo, w_s[:half, :], preferred_element_type=F32)
                    + jnp.dot(hi, w_s[half:, :], preferred_element_type=F32) + b_ref[0])

        gate = jnp.minimum(proj(wg_s, bg_ref), SWIGLU_LIMIT)
        up = jnp.clip(proj(wu_s, bu_ref), -SWIGLU_LIMIT, SWIGLU_LIMIT)
        glu = gate * jax.nn.sigmoid(SWIGLU_ALPHA * gate)
        hm_ref[:rows, :] = ((up + 1.0) * glu).astype(hm_ref.dtype)
        if rows < MOE_TM:
            hm_ref[rows:, :] = jnp.zeros((MOE_TM - rows, hm_ref.shape[1]), hm_ref.dtype)

    _for_rows_in_use(fl, compute)

    @pl.when((fl & FLAG_VALID) == 0)
    def _():
        hm_ref[...] = jnp.zeros_like(hm_ref)


def _expert_up(items, xs, wg, wu, bg, bu):
    n_rows, hw = xs.shape
    n_exp, d, f = wg.shape
    n_items = items[0].shape[0]
    b_spec = pl.BlockSpec((1, 1, MOE_TF), lambda q, rb, e, ft, *_: (e[q], 0, ft[q]))
    return pl.pallas_call(
        _expert_up_kernel,
        out_shape=jax.ShapeDtypeStruct((n_rows, f), BF16),
        grid_spec=pltpu.PrefetchScalarGridSpec(
            num_scalar_prefetch=len(items),
            grid=(n_items,),
            in_specs=[pl.BlockSpec((MOE_TM, hw), lambda q, rb, *_: (rb[q], 0)),
                      b_spec, b_spec,
                      pl.BlockSpec(memory_space=pl.ANY),
                      pl.BlockSpec(memory_space=pl.ANY)],
            out_specs=pl.BlockSpec((MOE_TM, MOE_TF),
                                   lambda q, rb, e, ft, orb, oft, *_: (orb[q], oft[q])),
            scratch_shapes=[pltpu.VMEM((d, MOE_TF), F32), pltpu.VMEM((d, MOE_TF), F32),
                            pltpu.VMEM((d, MOE_TF), BF16), pltpu.VMEM((d, MOE_TF), BF16),
                            pltpu.SemaphoreType.DMA((2,))],
        ),
        compiler_params=_cparams("arbitrary"),
        name="expert_up",
    )(*items, xs, bg.reshape(n_exp, 1, f), bu.reshape(n_exp, 1, f), wg, wu)


def _expert_down_kernel(bidx_ref, be_ref, flags, nxt_e, hm_ref, bd_ref, wd_hbm, y_ref,
                        stage, wd_s, sem):
    b = pl.program_id(0)
    fl = flags[b]
    half = y_ref.shape[1]

    def weight_copy(e):
        return pltpu.make_async_copy(wd_hbm.at[e], stage, sem)

    @pl.when(b == 0)
    def _():
        weight_copy(be_ref[0]).start()

    @pl.when((fl & FLAG_FIRST) != 0)
    def _():
        weight_copy(be_ref[b]).wait()
        wd_s[...] = stage[...].astype(BF16)

        @pl.when((fl & FLAG_HAS_NEXT) != 0)
        def _():
            weight_copy(nxt_e[b]).start()

    def compute(rows):
        y = jnp.dot(hm_ref[:rows, :], wd_s[...], preferred_element_type=F32) + bd_ref[0]
        y_ref[:rows, :] = _pack_bf16_pair(y[:, :half], y[:, half:])
        if rows < MOE_TM:
            y_ref[rows:, :] = jnp.zeros((MOE_TM - rows, half), y_ref.dtype)

    _for_rows_in_use(fl, compute)

    @pl.when((fl & FLAG_VALID) == 0)
    def _():
        y_ref[...] = jnp.zeros_like(y_ref)


def _expert_down(bidx, be, flags, nxt_e, hm, wd, bd):
    n_rows, f = hm.shape
    n_exp, _, d = wd.shape
    return pl.pallas_call(
        _expert_down_kernel,
        out_shape=jax.ShapeDtypeStruct((n_rows, d // 2), jnp.uint32),
        grid_spec=pltpu.PrefetchScalarGridSpec(
            num_scalar_prefetch=4,
            grid=(n_rows // MOE_TM,),
            in_specs=[pl.BlockSpec((MOE_TM, f), lambda b, bi, *_: (bi[b], 0)),
                      pl.BlockSpec((1, 1, d), lambda b, bi, be, *_: (be[b], 0, 0)),
                      pl.BlockSpec(memory_space=pl.ANY)],
            out_specs=pl.BlockSpec((MOE_TM, d // 2), lambda b, *_: (b, 0)),
            scratch_shapes=[pltpu.VMEM((f, d), F32), pltpu.VMEM((f, d), BF16),
                            pltpu.SemaphoreType.DMA(())],
        ),
        compiler_params=_cparams("arbitrary"),
        name="expert_down",
    )(bidx, be, flags, nxt_e, hm, bd.reshape(n_exp, 1, d), wd)


def _combine_kernel(dest_ref, dest_next_ref, x1_ref, w_ref, g_ref, y_ref, o_ref, buf_ref, sems):
    i = pl.program_id(0)
    nt = pl.num_programs(0)
    tm = x1_ref.shape[0]
    slot = i % 2

    def issue(d_ref, sl):
        def start(t, c):
            for k in range(TOP_K):
                pltpu.make_async_copy(y_ref.at[pl.ds(d_ref[0, k, t], 1)],
                                      buf_ref.at[sl, k, pl.ds(t, 1)], sems.at[sl]).start()
            return c
        lax.fori_loop(0, tm, start, 0, unroll=DMA_UNROLL)

    @pl.when(i == 0)
    def _():
        issue(dest_ref, 0)

    @pl.when(i + 1 < nt)
    def _():
        issue(dest_next_ref, 1 - slot)

    def wait(t, c):
        for k in range(TOP_K):
            pltpu.make_async_copy(y_ref.at[pl.ds(0, 1)], buf_ref.at[slot, 0, pl.ds(0, 1)],
                                  sems.at[slot]).wait()
        return c

    lax.fori_loop(0, tm, wait, 0, unroll=DMA_UNROLL)

    half = buf_ref.shape[-1]
    acc_lo = x1_ref[:, :half]
    acc_hi = x1_ref[:, half:]
    for k in range(TOP_K):
        words = buf_ref[slot, k]
        wk = w_ref[:, k:k + 1]
        acc_lo = acc_lo + wk * lax.bitcast_convert_type(words << 16, F32)
        acc_hi = acc_hi + wk * lax.bitcast_convert_type(words & jnp.uint32(0xFFFF0000), F32)
    ms = (jnp.sum(acc_lo * acc_lo, axis=-1, keepdims=True)
          + jnp.sum(acc_hi * acc_hi, axis=-1, keepdims=True)) / (2 * half)
    scale = lax.rsqrt(ms + EPS)
    o_ref[:, :half] = acc_lo * scale * g_ref[:, :half]
    o_ref[:, half:] = acc_hi * scale * g_ref[:, half:]


def _combine(dest, x1, topw_cols, g, y, tm=COMB_TM):
    n, d = x1.shape
    nt = n // tm
    return pl.pallas_call(
        _combine_kernel,
        out_shape=jax.ShapeDtypeStruct((n, d), F32),
        grid=(nt,),
        in_specs=[pl.BlockSpec((1, TOP_K, tm), lambda i: (i, 0, 0), memory_space=pltpu.SMEM),
                  pl.BlockSpec((1, TOP_K, tm), lambda i: (jnp.minimum(i + 1, nt - 1), 0, 0),
                               memory_space=pltpu.SMEM),
                  pl.BlockSpec((tm, d), lambda i: (i, 0)),
                  pl.BlockSpec((tm, TOP_K), lambda i: (i, 0)),
                  pl.BlockSpec((1, d), lambda i: (0, 0)),
                  pl.BlockSpec(memory_space=pl.ANY)],
        out_specs=pl.BlockSpec((tm, d), lambda i: (i, 0)),
        scratch_shapes=[pltpu.VMEM((2, TOP_K, tm, d // 2), jnp.uint32),
                        pltpu.SemaphoreType.DMA((2,))],
        compiler_params=_cparams("arbitrary"),
        name="combine",
    )(dest, dest, x1, topw_cols, g.reshape(1, d), y)


def _routing_tables(counts, topi, rank, n_blocks, n_ftiles):
    def lookup(table, idx):
        hit = idx[..., None] == jnp.arange(table.shape[0], dtype=I32)
        return jnp.sum(jnp.where(hit, table, 0), axis=-1).astype(I32)

    nb = (counts + MOE_TM - 1) // MOE_TM
    bend = jnp.cumsum(nb)
    bstart = bend - nb
    nused = bend[-1]
    pstart = bstart * MOE_TM
    dest = lookup(pstart, topi) + rank
    b = jnp.arange(n_blocks, dtype=I32)
    bc = jnp.minimum(b, nused - 1)
    be = jnp.sum(bend[None, :] <= bc[:, None], axis=1).astype(I32)
    bvalid = b < nused
    bstart_b = lookup(bstart, be)
    bfirst = bvalid & (bc == bstart_b)
    bnext = b + lookup(nb, be)
    b_nxt_e = lookup(be, jnp.minimum(bnext, n_blocks - 1))
    rows_used = jnp.clip(lookup(counts, be) - (bc - bstart_b) * MOE_TM, 1, MOE_TM)
    bcls = jnp.where(bvalid, (rows_used + MOE_ROW_Q - 1) // MOE_ROW_Q, 0)
    bflags = (FLAG_VALID * bvalid.astype(I32) + FLAG_FIRST * bfirst.astype(I32)
              + FLAG_HAS_NEXT * (bfirst & (bnext < nused)).astype(I32) + (bcls << FLAG_ROWS_SHIFT))
    q = jnp.arange(n_ftiles * n_blocks, dtype=I32)
    total = n_ftiles * nused
    qc = jnp.minimum(q, total - 1)
    qe = jnp.sum((n_ftiles * bend)[None, :] <= qc[:, None], axis=1).astype(I32)
    bstart_q = lookup(bstart, qe)
    local = qc - n_ftiles * bstart_q
    nbe = jnp.maximum(lookup(nb, qe), 1)
    in_f = local // nbe
    in_rb = bstart_q + local % nbe
    qvalid = q < total
    tail = jnp.maximum(q - total, 0)
    out_rb = jnp.where(qvalid, in_rb, nused + tail // n_ftiles)
    out_f = jnp.where(qvalid, in_f, tail % n_ftiles)
    qfirst = qvalid & (local % nbe == 0)
    qnext = q + nbe
    qn = jnp.minimum(qnext, q.shape[0] - 1)
    qcls = jnp.where(qvalid, lookup(bcls, in_rb), 0)
    qflags = (FLAG_VALID * qvalid.astype(I32) + FLAG_FIRST * qfirst.astype(I32)
              + FLAG_HAS_NEXT * (qfirst & (qnext < total)).astype(I32) + (qcls << FLAG_ROWS_SHIFT))
    items = tuple(a.astype(I32) for a in (in_rb, qe, in_f, out_rb, out_f, qflags,
                                          lookup(qe, qn), lookup(in_f, qn)))
    zstart = ((pstart + counts) // MOE_ROW_Q).astype(I32)
    npad = jnp.where(nb * MOE_TM > counts, bend * (MOE_TM // MOE_ROW_Q) - zstart, 0).astype(I32)
    blocks = tuple(a.astype(I32) for a in (bc, be, bflags, b_nxt_e))
    return dest, zstart, npad, nused.astype(I32).reshape(1), blocks, items


def kernel(x, norm1_g, w_in, b_gates, w_pool_out, pool_scale, sgu_norm_g, w_spatial, b_spatial,
           w_sgu_out, w_out, norm2_g, w_router, b_router, w_gate, b_gate, w_up, b_up, w_down,
           b_down, normf_g):
    bsz, seq, d = x.shape
    n = bsz * seq
    groups, gd, _ = w_pool_out.shape
    pool_w = groups * gd
    sgu_w = w_sgu_out.shape[0]
    n_exp = w_router.shape[1]
    xf = x.reshape(n, d)

    h, pooled = _inproj_pool(xf, norm1_g, w_in[:, :pool_w].astype(BF16), groups, gd, seq)
    s = _inproj_sgu(h, w_in, pool_w, pool_w + sgu_w, sgu_w, sgu_norm_g, w_spatial, b_spatial)
    m = _mix(h, w_in, pool_w + 2 * sgu_w, b_gates, pooled, w_pool_out.astype(BF16), pool_scale,
             s, w_sgu_out)
    x1 = _outproj(m, w_out, xf)

    h2, topi, rank, topw, counts = _router(x1, norm2_g, w_router, b_router)

    n_blocks = (n * TOP_K) // MOE_TM + n_exp
    n_ftiles = w_gate.shape[2] // MOE_TF
    dest, zstart, npad, nused, blocks, items = _routing_tables(
        counts[:, 0], topi, rank, n_blocks, n_ftiles)

    xs = _dispatch(zstart, npad, nused, dest, h2, n_blocks * MOE_TM)
    hm = _expert_up(items, xs, w_gate, w_up, b_gate, b_up)
    y = _expert_down(*blocks, hm, w_down, b_down)
    assert COMB_TM == TOK_TM
    out = _combine(dest, x1, topw.T, normf_g, y)
    return out.reshape(bsz, seq, d)
```

```python
import functools
import math

import jax
import jax.numpy as jnp
from jax import lax
from jax.experimental import pallas as pl
from jax.experimental.pallas import tpu as pltpu

CHUNK = 64
POOL_WINDOWS = (2, 4, 8, 16)
SGU_HEADS = 8
SGU_BLOCK = 128
TOP_K = 4
SWIGLU_LIMIT = 7.0
SWIGLU_ALPHA = 1.702
EPS = 1e-5

VMEM_LIMIT_BYTES = 58 * 1024 * 1024
POOL_HALO = 16
MOE_TM = 512
MOE_ROW_Q = 128
MOE_TF = 512
TOK_TM = 256
COMB_TM = 256
DMA_UNROLL = 8

BF16 = jnp.bfloat16
F32 = jnp.float32
I32 = jnp.int32


def _cparams(*sem):
    return pltpu.CompilerParams(dimension_semantics=sem, vmem_limit_bytes=VMEM_LIMIT_BYTES)


def _gelu_tanh(x):
    c = math.sqrt(2.0 / math.pi)
    return x * (0.5 * (1.0 + jnp.tanh(c * (x + 0.044715 * (x * x * x)))))


def _rms_scale(x):
    return lax.rsqrt(jnp.mean(x * x, axis=-1, keepdims=True) + EPS)


def _inproj_pool_kernel(x_ref, g_ref, w_ref, h_ref, o_ref, buf_ref, carry_ref, *, seq):
    i = pl.program_id(0)
    tm = x_ref.shape[0]
    gd = buf_ref.shape[-1]
    x = x_ref[...]
    h = (x * _rms_scale(x) * g_ref[...]).astype(h_ref.dtype)
    h_ref[...] = h
    seq_pos0 = (i * tm) % seq
    t1 = seq_pos0 + 1 + lax.broadcasted_iota(I32, (tm, 1), 0)

    for gi, win in enumerate(POOL_WINDOWS):
        cols = slice(gi * gd, (gi + 1) * gd)
        p = jnp.dot(h, w_ref[:, cols], preferred_element_type=F32)
        buf_ref[gi, 0:POOL_HALO, :] = jnp.where(seq_pos0 == 0, 0.0, carry_ref[gi])
        buf_ref[gi, POOL_HALO:, :] = p
        carry_ref[gi] = p[tm - POOL_HALO:, :]
        s = buf_ref[gi]
        shift = 1
        while shift < win:
            s = s + pltpu.roll(s, shift, 0)
            shift *= 2
        denom = jnp.minimum(t1, win).astype(F32)
        o_ref[:, cols] = (s[POOL_HALO:, :] / denom - p).astype(o_ref.dtype)


def _inproj_pool(x, g, w, n_groups, gd, seq, tm=512):
    n, d = x.shape
    assert seq % tm == 0 and n_groups == len(POOL_WINDOWS) and max(POOL_WINDOWS) <= POOL_HALO
    pw = n_groups * gd
    return pl.pallas_call(
        functools.partial(_inproj_pool_kernel, seq=seq),
        out_shape=(jax.ShapeDtypeStruct((n, d), BF16),
                   jax.ShapeDtypeStruct((n, pw), BF16)),
        grid=(n // tm,),
        in_specs=[pl.BlockSpec((tm, d), lambda i: (i, 0)),
                  pl.BlockSpec((1, d), lambda i: (0, 0)),
                  pl.BlockSpec((d, pw), lambda i: (0, 0), pipeline_mode=pl.Buffered(1))],
        out_specs=(pl.BlockSpec((tm, d), lambda i: (i, 0)),
                   pl.BlockSpec((tm, pw), lambda i: (i, 0))),
        scratch_shapes=[pltpu.VMEM((n_groups, POOL_HALO + tm, gd), F32),
                        pltpu.VMEM((n_groups, POOL_HALO, gd), F32)],
        compiler_params=_cparams("arbitrary"),
        name="inproj_pool",
    )(x, g.reshape(1, d), w)


def _stage_weight_tiles(w_hbm, col0s, tn, stages, casts, sems):
    j = pl.program_id(0)
    i = pl.program_id(1)

    def copies(jj):
        return [pltpu.make_async_copy(
                    w_hbm.at[:, pl.ds(pl.multiple_of(c0 + jj * tn, tn), tn)], st, sems.at[k])
                for k, (c0, st) in enumerate(zip(col0s, stages))]

    @pl.when((j == 0) & (i == 0))
    def _():
        for c in copies(0):
            c.start()

    @pl.when(i == 0)
    def _():
        for c in copies(j):
            c.wait()
        for st, dst in zip(stages, casts):
            dst[...] = st[...].astype(dst.dtype)

        @pl.when(j + 1 < pl.num_programs(0))
        def _():
            for c in copies(j + 1):
                c.start()


def _inproj_sgu_kernel(h_ref, g_ref, ws_ref, b_ref, w_hbm, s_ref, stage_u, stage_v, wu_ref, wv_ref,
                       sems, *, head_dim, u_col0, v_col0):
    _stage_weight_tiles(w_hbm, (u_col0, v_col0), s_ref.shape[1], (stage_u, stage_v),
                        (wu_ref, wv_ref), sems)
    h = h_ref[...]
    u = _gelu_tanh(jnp.dot(h, wu_ref[...], preferred_element_type=F32))
    v = _gelu_tanh(jnp.dot(h, wv_ref[...], preferred_element_type=F32))
    pos_i = lax.broadcasted_iota(I32, (SGU_BLOCK, SGU_BLOCK), 0)
    pos_j = lax.broadcasted_iota(I32, (SGU_BLOCK, SGU_BLOCK), 1)
    readable = (pos_j // CHUNK) <= (pos_i // CHUNK)
    for hh in range(u.shape[1] // head_dim):
        cols = slice(hh * head_dim, (hh + 1) * head_dim)
        blk = v[:, cols]
        vn = (blk * _rms_scale(blk) * g_ref[:, cols]).astype(BF16)
        ws = jnp.where(readable, ws_ref[hh], 0.0).astype(BF16)
        bias = b_ref[hh]
        for c in range(u.shape[0] // SGU_BLOCK):
            rows = slice(c * SGU_BLOCK, (c + 1) * SGU_BLOCK)
            mixed = jnp.dot(ws, vn[rows, :], preferred_element_type=F32) + bias
            s_ref[rows, cols] = (u[rows, cols] * mixed).astype(s_ref.dtype)


def _inproj_sgu(h, w, u_col0, v_col0, width, norm_g, w_spatial, b_spatial, tm=1024, tn=512):
    n, d = h.shape
    hd = width // SGU_HEADS
    hpt = tn // hd
    assert u_col0 % tn == 0 and v_col0 % tn == 0
    return pl.pallas_call(
        functools.partial(_inproj_sgu_kernel, head_dim=hd, u_col0=u_col0, v_col0=v_col0),
        out_shape=jax.ShapeDtypeStruct((n, width), BF16),
        grid=(width // tn, n // tm),
        in_specs=[pl.BlockSpec((tm, d), lambda j, i: (i, 0)),
                  pl.BlockSpec((1, tn), lambda j, i: (0, j)),
                  pl.BlockSpec((hpt, SGU_BLOCK, SGU_BLOCK), lambda j, i: (j, 0, 0)),
                  pl.BlockSpec((hpt, SGU_BLOCK, 1), lambda j, i: (j, 0, 0)),
                  pl.BlockSpec(memory_space=pl.ANY)],
        out_specs=pl.BlockSpec((tm, tn), lambda j, i: (i, j)),
        scratch_shapes=[pltpu.VMEM((d, tn), F32), pltpu.VMEM((d, tn), F32),
                        pltpu.VMEM((d, tn), BF16), pltpu.VMEM((d, tn), BF16),
                        pltpu.SemaphoreType.DMA((2,))],
        compiler_params=_cparams("arbitrary", "arbitrary"),
        name="inproj_sgu",
    )(h, norm_g.reshape(1, width), w_spatial, b_spatial.reshape(SGU_HEADS, SGU_BLOCK, 1), w)


def _mix_kernel(h_ref, ba_ref, bb_ref, pooled_ref, wp_ref, scale_ref, s_ref, w_hbm, wsgu_hbm,
                m_ref, stage_a, stage_b, stage_s, wga_ref, wgb_ref, wsgu_ref, sems, sems_s,
                *, ga_col0, gb_col0):
    tn = m_ref.shape[1]
    _stage_weight_tiles(w_hbm, (ga_col0, gb_col0), tn, (stage_a, stage_b), (wga_ref, wgb_ref), sems)
    _stage_weight_tiles(wsgu_hbm, (0,), tn, (stage_s,), (wsgu_ref,), sems_s)
    h = h_ref[...]
    g_a = jax.nn.sigmoid(jnp.dot(h, wga_ref[...], preferred_element_type=F32) + ba_ref[...])
    y_a = jnp.dot(pooled_ref[...], wp_ref[0], preferred_element_type=F32) * scale_ref[...]
    m = g_a * y_a
    g_b = jax.nn.sigmoid(jnp.dot(h, wgb_ref[...], preferred_element_type=F32) + bb_ref[...])
    y_b = jnp.dot(s_ref[...], wsgu_ref[...], preferred_element_type=F32)
    m_ref[...] = (m + g_b * y_b).astype(m_ref.dtype)


def _mix(h, w_in, gate_col0, b_gates, pooled, w_pool, pool_scale, s, w_sgu, tm=512, tn=512):
    n, d = h.shape
    groups, gd, od = w_pool.shape
    sw = s.shape[1]
    tpg = od // tn
    nd = d // tn
    assert gate_col0 % tn == 0
    return pl.pallas_call(
        functools.partial(_mix_kernel, ga_col0=gate_col0, gb_col0=gate_col0 + d),
        out_shape=jax.ShapeDtypeStruct((n, d), BF16),
        grid=(nd, n // tm),
        in_specs=[pl.BlockSpec((tm, d), lambda j, i: (i, 0)),
                  pl.BlockSpec((1, tn), lambda j, i: (0, j)),
                  pl.BlockSpec((1, tn), lambda j, i: (0, j + nd)),
                  pl.BlockSpec((tm, gd), lambda j, i: (i, j // tpg)),
                  pl.BlockSpec((1, gd, tn), lambda j, i: (j // tpg, 0, j % tpg)),
                  pl.BlockSpec((1, tn), lambda j, i: (0, j)),
                  pl.BlockSpec((tm, sw), lambda j, i: (i, 0)),
                  pl.BlockSpec(memory_space=pl.ANY),
                  pl.BlockSpec(memory_space=pl.ANY)],
        out_specs=pl.BlockSpec((tm, tn), lambda j, i: (i, j)),
        scratch_shapes=[pltpu.VMEM((d, tn), F32), pltpu.VMEM((d, tn), F32), pltpu.VMEM((sw, tn), F32),
                        pltpu.VMEM((d, tn), BF16), pltpu.VMEM((d, tn), BF16),
                        pltpu.VMEM((sw, tn), BF16),
                        pltpu.SemaphoreType.DMA((2,)), pltpu.SemaphoreType.DMA((1,))],
        compiler_params=_cparams("arbitrary", "arbitrary"),
        name="mix",
    )(h, b_gates.reshape(1, 2 * d), b_gates.reshape(1, 2 * d), pooled, w_pool,
      pool_scale.reshape(1, d), s, w_in, w_sgu)


def _outproj_kernel(m_ref, x_ref, w_hbm, o_ref, stage, w_ref, sems):
    _stage_weight_tiles(w_hbm, (0,), o_ref.shape[1], (stage,), (w_ref,), sems)
    o_ref[...] = x_ref[...] + jnp.dot(m_ref[...], w_ref[...], preferred_element_type=F32)


def _outproj(m, w, x, tm=512, tn=1024):
    n, d = m.shape
    return pl.pallas_call(
        _outproj_kernel,
        out_shape=jax.ShapeDtypeStruct((n, d), F32),
        grid=(d // tn, n // tm),
        in_specs=[pl.BlockSpec((tm, d), lambda j, i: (i, 0)),
                  pl.BlockSpec((tm, tn), lambda j, i: (i, j)),
                  pl.BlockSpec(memory_space=pl.ANY)],
        out_specs=pl.BlockSpec((tm, tn), lambda j, i: (i, j)),
        scratch_shapes=[pltpu.VMEM((d, tn), F32), pltpu.VMEM((d, tn), BF16),
                        pltpu.SemaphoreType.DMA((1,))],
        compiler_params=_cparams("arbitrary", "arbitrary"),
        name="outproj",
    )(m, x, w)


def _pack_bf16_pair(lo, hi):
    lo_bits = lax.bitcast_convert_type(lo.astype(BF16).astype(F32), jnp.uint32)
    hi_bits = lax.bitcast_convert_type(hi.astype(BF16).astype(F32), jnp.uint32)
    return (hi_bits & jnp.uint32(0xFFFF0000)) | (lo_bits >> 16)


def _unpack_bf16_pair(words):
    lo = lax.bitcast_convert_type(words << 16, F32).astype(BF16)
    hi = lax.bitcast_convert_type(words & jnp.uint32(0xFFFF0000), F32).astype(BF16)
    return lo, hi


def _router_kernel(x_ref, g_ref, wr_ref, br_ref, h2_ref, topi_ref, rank_ref, topw_ref, cnt_ref,
                   carry_ref):
    i = pl.program_id(0)
    tm, d = x_ref.shape
    n_exp = wr_ref.shape[0]
    half = d // 2

    @pl.when(i == 0)
    def _():
        carry_ref[...] = jnp.zeros_like(carry_ref)

    x = x_ref[...]
    h2 = x * _rms_scale(x) * g_ref[...]
    h2_ref[...] = _pack_bf16_pair(h2[:, :half], h2[:, half:])

    logits = lax.dot_general(wr_ref[...].astype(BF16), h2.astype(BF16),
                             (((1,), (1,)), ((), ())), preferred_element_type=F32)
    logits = logits + br_ref[...]

    e_iota = lax.broadcasted_iota(I32, (n_exp, tm), 0)
    vals = logits
    top_v, sels = [], []
    for k in range(TOP_K):
        mx = jnp.max(vals, axis=0, keepdims=True)
        idx = jnp.min(jnp.where(vals == mx, e_iota, n_exp), axis=0, keepdims=True)
        sel = e_iota == idx
        vals = jnp.where(sel, -jnp.inf, vals)
        top_v.append(mx)
        sels.append(sel)
        topi_ref[0, k:k + 1, :] = idx

    exps = [jnp.exp(v - top_v[0]) for v in top_v]
    denom = exps[0] + exps[1] + exps[2] + exps[3]
    for k in range(TOP_K):
        topw_ref[k:k + 1, :] = exps[k] / denom

    chosen = jnp.zeros((n_exp, tm), F32)
    for sel in sels:
        chosen = chosen + sel.astype(F32)
    earlier = (lax.broadcasted_iota(I32, (tm, tm), 0)
               < lax.broadcasted_iota(I32, (tm, tm), 1)).astype(BF16)
    before = jnp.dot(chosen.astype(BF16), earlier, preferred_element_type=F32) + carry_ref[...]
    for k in range(TOP_K):
        r = jnp.sum(jnp.where(sels[k], before, 0.0), axis=0, keepdims=True)
        rank_ref[0, k:k + 1, :] = r.astype(I32)
    carry_ref[...] = carry_ref[...] + jnp.sum(chosen, axis=1, keepdims=True)
    cnt_ref[...] = carry_ref[...].astype(I32)


def _router(x1, g, w_router, b_router, tm=TOK_TM):
    n, d = x1.shape
    n_exp = w_router.shape[1]
    nt = n // tm
    return pl.pallas_call(
        _router_kernel,
        out_shape=(jax.ShapeDtypeStruct((n, d // 2), jnp.uint32),
                   jax.ShapeDtypeStruct((nt, TOP_K, tm), I32),
                   jax.ShapeDtypeStruct((nt, TOP_K, tm), I32),
                   jax.ShapeDtypeStruct((TOP_K, n), F32),
                   jax.ShapeDtypeStruct((n_exp, 1), I32)),
        grid=(nt,),
        in_specs=[pl.BlockSpec((tm, d), lambda i: (i, 0)),
                  pl.BlockSpec((1, d), lambda i: (0, 0)),
                  pl.BlockSpec((n_exp, d), lambda i: (0, 0)),
                  pl.BlockSpec((n_exp, 1), lambda i: (0, 0))],
        out_specs=(pl.BlockSpec((tm, d // 2), lambda i: (i, 0)),
                   pl.BlockSpec((1, TOP_K, tm), lambda i: (i, 0, 0)),
                   pl.BlockSpec((1, TOP_K, tm), lambda i: (i, 0, 0)),
                   pl.BlockSpec((TOP_K, tm), lambda i: (0, i)),
                   pl.BlockSpec((n_exp, 1), lambda i: (0, 0))),
        scratch_shapes=[pltpu.VMEM((n_exp, 1), F32)],
        compiler_params=_cparams("arbitrary"),
        name="router",
    )(x1, g.reshape(1, d), w_router.T, b_router.reshape(n_exp, 1))


def _dispatch_kernel(zstart_ref, npad_ref, nused_ref, dest_ref, h2_ref, xs_ref, zero_ref, sem):
    i = pl.program_id(0)
    tm = h2_ref.shape[0]
    n_exp = zstart_ref.shape[0]
    qpb = MOE_TM // MOE_ROW_Q
    n_chunks = xs_ref.shape[0] // MOE_ROW_Q

    @pl.when(i == 0)
    def _():
        zero_ref[...] = jnp.zeros_like(zero_ref)

        def zero_copy(chunk):
            row0 = pl.multiple_of(chunk * MOE_ROW_Q, MOE_ROW_Q)
            return pltpu.make_async_copy(zero_ref, xs_ref.at[pl.ds(row0, MOE_ROW_Q)], sem)

        def for_pad_chunks(fn):
            def per_expert(e, c):
                lax.fori_loop(zstart_ref[e], zstart_ref[e] + npad_ref[e],
                              lambda ch, cc: (fn(ch), cc)[1], 0)
                return c
            lax.fori_loop(0, n_exp, per_expert, 0)

        def for_tail_chunks(fn):
            lax.fori_loop(nused_ref[0] * qpb, n_chunks, lambda ch, cc: (fn(ch), cc)[1], 0)

        for_pad_chunks(lambda ch: zero_copy(ch).start())
        for_tail_chunks(lambda ch: zero_copy(ch).start())
        for_pad_chunks(lambda ch: zero_copy(ch).wait())
        for_tail_chunks(lambda ch: zero_copy(ch).wait())

    def start(t, c):
        for k in range(TOP_K):
            pltpu.make_async_copy(h2_ref.at[pl.ds(t, 1)],
                                  xs_ref.at[pl.ds(dest_ref[0, k, t], 1)], sem).start()
        return c

    def wait(t, c):
        for k in range(TOP_K):
            pltpu.make_async_copy(h2_ref.at[pl.ds(0, 1)], xs_ref.at[pl.ds(0, 1)], sem).wait()
        return c

    lax.fori_loop(0, tm, start, 0, unroll=DMA_UNROLL)
    lax.fori_loop(0, tm, wait, 0, unroll=DMA_UNROLL)


def _dispatch(zstart, npad, nused, dest, h2, n_rows, tm=TOK_TM):
    n, hw = h2.shape
    return pl.pallas_call(
        _dispatch_kernel,
        out_shape=jax.ShapeDtypeStruct((n_rows, hw), jnp.uint32),
        grid_spec=pltpu.PrefetchScalarGridSpec(
            num_scalar_prefetch=3,
            grid=(n // tm,),
            in_specs=[pl.BlockSpec((1, TOP_K, tm), lambda i, *_: (i, 0, 0), memory_space=pltpu.SMEM),
                      pl.BlockSpec((tm, hw), lambda i, *_: (i, 0))],
            out_specs=pl.BlockSpec(memory_space=pl.ANY),
            scratch_shapes=[pltpu.VMEM((MOE_ROW_Q, hw), jnp.uint32),
                            pltpu.SemaphoreType.DMA(())],
        ),
        compiler_params=_cparams("arbitrary"),
        name="dispatch",
    )(zstart, npad, nused, dest, h2)


FLAG_VALID, FLAG_FIRST, FLAG_HAS_NEXT = 1, 2, 4
FLAG_ROWS_SHIFT = 3


def _expert_up_kernel(in_rb, in_e, in_f, out_rb, out_f, flags, nxt_e, nxt_f,
                      xs_ref, bg_ref, bu_ref, wg_hbm, wu_hbm, hm_ref,
                      stage_g, stage_u, wg_s, wu_s, sems):
    q = pl.program_id(0)
    fl = flags[q]
    is_first = (fl & FLAG_FIRST) != 0

    def weight_copies(e, f):
        cols = pl.ds(pl.multiple_of(f * MOE_TF, MOE_TF), MOE_TF)
        return (pltpu.make_async_copy(wg_hbm.at[e, :, cols], stage_g, sems.at[0]),
                pltpu.make_async_copy(wu_hbm.at[e, :, cols], stage_u, sems.at[1]))

    @pl.when(q == 0)
    def _():
        for c in weight_copies(in_e[0], in_f[0]):
            c.start()

    @pl.when(is_first)
    def _():
        for c in weight_copies(in_e[q], in_f[q]):
            c.wait()

    def compute(first, rows):
        half = xs_ref.shape[1]
        lo, hi = _unpack_bf16_pair(xs_ref[:rows, :])
        if first:
            kc = half // 2
            gate = up = None
            for c in range(4):
                xk = (lo if c < 2 else hi)[:, (c % 2) * kc:(c % 2 + 1) * kc]
                wrows = slice(c * kc, (c + 1) * kc)
                wgc = stage_g[wrows, :].astype(BF16)
                wuc = stage_u[wrows, :].astype(BF16)
                wg_s[wrows, :] = wgc
                wu_s[wrows, :] = wuc
                dg = jnp.dot(xk, wgc, preferred_element_type=F32)
                du = jnp.dot(xk, wuc, preferred_element_type=F32)
                gate = dg if gate is None else gate + dg
                up = du if up is None else up + du
            gate = gate + bg_ref[0]
            up = up + bu_ref[0]
        else:
            def proj(w_s, b_ref):
                return (jnp.dot(lo, w_s[:half, :], preferred_element_type=F32)
                        + jnp.dot(hi, w_s[half:, :], preferred_element_type=F32) + b_ref[0])

            gate = proj(wg_s, bg_ref)
            up = proj(wu_s, bu_ref)
        gate = jnp.minimum(gate, SWIGLU_LIMIT)
        up = jnp.clip(up, -SWIGLU_LIMIT, SWIGLU_LIMIT)
        glu = gate * jax.nn.sigmoid(SWIGLU_ALPHA * gate)
        hm_ref[:rows, :] = ((up + 1.0) * glu).astype(hm_ref.dtype)
        if rows < MOE_TM:
            hm_ref[rows:, :] = jnp.zeros((MOE_TM - rows, hm_ref.shape[1]), hm_ref.dtype)

    cls = fl >> FLAG_ROWS_SHIFT
    for c in range(1, MOE_TM // MOE_ROW_Q + 1):
        pl.when((cls == c) & is_first)(functools.partial(compute, True, c * MOE_ROW_Q))
        pl.when((cls == c) & jnp.logical_not(is_first))(
            functools.partial(compute, False, c * MOE_ROW_Q))

    @pl.when(is_first & ((fl & FLAG_HAS_NEXT) != 0))
    def _():
        for c in weight_copies(nxt_e[q], nxt_f[q]):
            c.start()

    @pl.when((fl & FLAG_VALID) == 0)
    def _():
        hm_ref[...] = jnp.zeros_like(hm_ref)


def _expert_up(items, xs, wg, wu, bg, bu):
    n_rows, hw = xs.shape
    n_exp, d, f = wg.shape
    n_items = items[0].shape[0]
    b_spec = pl.BlockSpec((1, 1, MOE_TF), lambda q, rb, e, ft, *_: (e[q], 0, ft[q]))
    return pl.pallas_call(
        _expert_up_kernel,
        out_shape=jax.ShapeDtypeStruct((n_rows, f), BF16),
        grid_spec=pltpu.PrefetchScalarGridSpec(
            num_scalar_prefetch=len(items),
            grid=(n_items,),
            in_specs=[pl.BlockSpec((MOE_TM, hw), lambda q, rb, *_: (rb[q], 0)),
                      b_spec, b_spec,
                      pl.BlockSpec(memory_space=pl.ANY),
                      pl.BlockSpec(memory_space=pl.ANY)],
            out_specs=pl.BlockSpec((MOE_TM, MOE_TF),
                                   lambda q, rb, e, ft, orb, oft, *_: (orb[q], oft[q])),
            scratch_shapes=[pltpu.VMEM((d, MOE_TF), F32), pltpu.VMEM((d, MOE_TF), F32),
                            pltpu.VMEM((d, MOE_TF), BF16), pltpu.VMEM((d, MOE_TF), BF16),
                            pltpu.SemaphoreType.DMA((2,))],
        ),
        compiler_params=_cparams("arbitrary"),
        name="expert_up",
    )(*items, xs, bg.reshape(n_exp, 1, f), bu.reshape(n_exp, 1, f), wg, wu)


def _expert_down_kernel(bidx_ref, be_ref, flags, nxt_e, hm_ref, bd_ref, wd_hbm, y_ref,
                        stage, wd_s, sem):
    b = pl.program_id(0)
    fl = flags[b]
    half = y_ref.shape[1]

    def weight_copy(e):
        return pltpu.make_async_copy(wd_hbm.at[e], stage, sem)

    @pl.when(b == 0)
    def _():
        weight_copy(be_ref[0]).start()

    is_first = (fl & FLAG_FIRST) != 0

    @pl.when(is_first)
    def _():
        weight_copy(be_ref[b]).wait()

    def compute(first, rows):
        hm = hm_ref[:rows, :]
        if first:
            hq = half // 2
            for a in (0, hq):
                lo_c, hi_c = slice(a, a + hq), slice(half + a, half + a + hq)
                w_lo = stage[:, lo_c].astype(BF16)
                w_hi = stage[:, hi_c].astype(BF16)
                wd_s[:, lo_c] = w_lo
                wd_s[:, hi_c] = w_hi
                y_lo = jnp.dot(hm, w_lo, preferred_element_type=F32) + bd_ref[0][:, lo_c]
                y_hi = jnp.dot(hm, w_hi, preferred_element_type=F32) + bd_ref[0][:, hi_c]
                y_ref[:rows, lo_c] = _pack_bf16_pair(y_lo, y_hi)
        else:
            y = jnp.dot(hm, wd_s[...], preferred_element_type=F32) + bd_ref[0]
            y_ref[:rows, :] = _pack_bf16_pair(y[:, :half], y[:, half:])
        if rows < MOE_TM:
            y_ref[rows:, :] = jnp.zeros((MOE_TM - rows, half), y_ref.dtype)

    cls = fl >> FLAG_ROWS_SHIFT
    for c in range(1, MOE_TM // MOE_ROW_Q + 1):
        pl.when((cls == c) & is_first)(functools.partial(compute, True, c * MOE_ROW_Q))
        pl.when((cls == c) & jnp.logical_not(is_first))(
            functools.partial(compute, False, c * MOE_ROW_Q))

    @pl.when(is_first & ((fl & FLAG_HAS_NEXT) != 0))
    def _():
        weight_copy(nxt_e[b]).start()

    @pl.when((fl & FLAG_VALID) == 0)
    def _():
        y_ref[...] = jnp.zeros_like(y_ref)


def _expert_down(bidx, be, flags, nxt_e, hm, wd, bd):
    n_rows, f = hm.shape
    n_exp, _, d = wd.shape
    return pl.pallas_call(
        _expert_down_kernel,
        out_shape=jax.ShapeDtypeStruct((n_rows, d // 2), jnp.uint32),
        grid_spec=pltpu.PrefetchScalarGridSpec(
            num_scalar_prefetch=4,
            grid=(n_rows // MOE_TM,),
            in_specs=[pl.BlockSpec((MOE_TM, f), lambda b, bi, *_: (bi[b], 0)),
                      pl.BlockSpec((1, 1, d), lambda b, bi, be, *_: (be[b], 0, 0)),
                      pl.BlockSpec(memory_space=pl.ANY)],
            out_specs=pl.BlockSpec((MOE_TM, d // 2), lambda b, *_: (b, 0)),
            scratch_shapes=[pltpu.VMEM((f, d), F32), pltpu.VMEM((f, d), BF16),
                            pltpu.SemaphoreType.DMA(())],
        ),
        compiler_params=_cparams("arbitrary"),
        name="expert_down",
    )(bidx, be, flags, nxt_e, hm, bd.reshape(n_exp, 1, d), wd)


def _combine_kernel(dest_ref, dest_next_ref, x1_ref, w_ref, g_ref, y_ref, o_ref, buf_ref, sems):
    i = pl.program_id(0)
    nt = pl.num_programs(0)
    tm = x1_ref.shape[0]
    slot = i % 2

    def issue(d_ref, sl):
        def start(t, c):
            for k in range(TOP_K):
                pltpu.make_async_copy(y_ref.at[pl.ds(d_ref[0, k, t], 1)],
                                      buf_ref.at[sl, k, pl.ds(t, 1)], sems.at[sl]).start()
            return c
        lax.fori_loop(0, tm, start, 0, unroll=DMA_UNROLL)

    @pl.when(i == 0)
    def _():
        issue(dest_ref, 0)

    @pl.when(i + 1 < nt)
    def _():
        issue(dest_next_ref, 1 - slot)

    def wait(t, c):
        for k in range(TOP_K):
            pltpu.make_async_copy(y_ref.at[pl.ds(0, 1)], buf_ref.at[slot, 0, pl.ds(0, 1)],
                                  sems.at[slot]).wait()
        return c

    lax.fori_loop(0, tm, wait, 0, unroll=DMA_UNROLL)

    half = buf_ref.shape[-1]
    acc_lo = x1_ref[:, :half]
    acc_hi = x1_ref[:, half:]
    for k in range(TOP_K):
        words = buf_ref[slot, k]
        wk = w_ref[:, k:k + 1]
        acc_lo = acc_lo + wk * lax.bitcast_convert_type(words << 16, F32)
        acc_hi = acc_hi + wk * lax.bitcast_convert_type(words & jnp.uint32(0xFFFF0000), F32)
    ms = (jnp.sum(acc_lo * acc_lo, axis=-1, keepdims=True)
          + jnp.sum(acc_hi * acc_hi, axis=-1, keepdims=True)) / (2 * half)
    scale = lax.rsqrt(ms + EPS)
    o_ref[:, :half] = acc_lo * scale * g_ref[:, :half]
    o_ref[:, half:] = acc_hi * scale * g_ref[:, half:]


def _combine(dest, x1, topw_cols, g, y, tm=COMB_TM):
    n, d = x1.shape
    nt = n // tm
    return pl.pallas_call(
        _combine_kernel,
        out_shape=jax.ShapeDtypeStruct((n, d), F32),
        grid=(nt,),
        in_specs=[pl.BlockSpec((1, TOP_K, tm), lambda i: (i, 0, 0), memory_space=pltpu.SMEM),
                  pl.BlockSpec((1, TOP_K, tm), lambda i: (jnp.minimum(i + 1, nt - 1), 0, 0),
                               memory_space=pltpu.SMEM),
                  pl.BlockSpec((tm, d), lambda i: (i, 0)),
                  pl.BlockSpec((tm, TOP_K), lambda i: (i, 0)),
                  pl.BlockSpec((1, d), lambda i: (0, 0)),
                  pl.BlockSpec(memory_space=pl.ANY)],
        out_specs=pl.BlockSpec((tm, d), lambda i: (i, 0)),
        scratch_shapes=[pltpu.VMEM((2, TOP_K, tm, d // 2), jnp.uint32),
                        pltpu.SemaphoreType.DMA((2,))],
        compiler_params=_cparams("arbitrary"),
        name="combine",
    )(dest, dest, x1, topw_cols, g.reshape(1, d), y)


def _routing_tables(counts, topi, rank, n_blocks, n_ftiles):
    def lookup(table, idx):
        hit = idx[..., None] == jnp.arange(table.shape[0], dtype=I32)
        return jnp.sum(jnp.where(hit, table, 0), axis=-1).astype(I32)

    nb = (counts + MOE_TM - 1) // MOE_TM
    bend = jnp.cumsum(nb)
    bstart = bend - nb
    nused = bend[-1]
    pstart = bstart * MOE_TM
    dest = lookup(pstart, topi) + rank
    b = jnp.arange(n_blocks, dtype=I32)
    bc = jnp.minimum(b, nused - 1)
    be = jnp.sum(bend[None, :] <= bc[:, None], axis=1).astype(I32)
    bvalid = b < nused
    bstart_b = lookup(bstart, be)
    bfirst = bvalid & (bc == bstart_b)
    bnext = b + lookup(nb, be)
    b_nxt_e = lookup(be, jnp.minimum(bnext, n_blocks - 1))
    rows_used = jnp.clip(lookup(counts, be) - (bc - bstart_b) * MOE_TM, 1, MOE_TM)
    bcls = jnp.where(bvalid, (rows_used + MOE_ROW_Q - 1) // MOE_ROW_Q, 0)
    bflags = (FLAG_VALID * bvalid.astype(I32) + FLAG_FIRST * bfirst.astype(I32)
              + FLAG_HAS_NEXT * (bfirst & (bnext < nused)).astype(I32) + (bcls << FLAG_ROWS_SHIFT))
    q = jnp.arange(n_ftiles * n_blocks, dtype=I32)
    total = n_ftiles * nused
    qc = jnp.minimum(q, total - 1)
    qe = jnp.sum((n_ftiles * bend)[None, :] <= qc[:, None], axis=1).astype(I32)
    bstart_q = lookup(bstart, qe)
    local = qc - n_ftiles * bstart_q
    nbe = jnp.maximum(lookup(nb, qe), 1)
    in_f = local // nbe
    in_rb = bstart_q + local % nbe
    qvalid = q < total
    tail = jnp.maximum(q - total, 0)
    out_rb = jnp.where(qvalid, in_rb, nused + tail // n_ftiles)
    out_f = jnp.where(qvalid, in_f, tail % n_ftiles)
    qfirst = qvalid & (local % nbe == 0)
    qnext = q + nbe
    qn = jnp.minimum(qnext, q.shape[0] - 1)
    qcls = jnp.where(qvalid, lookup(bcls, in_rb), 0)
    qflags = (FLAG_VALID * qvalid.astype(I32) + FLAG_FIRST * qfirst.astype(I32)
              + FLAG_HAS_NEXT * (qfirst & (qnext < total)).astype(I32) + (qcls << FLAG_ROWS_SHIFT))
    items = tuple(a.astype(I32) for a in (in_rb, qe, in_f, out_rb, out_f, qflags,
                                          lookup(qe, qn), lookup(in_f, qn)))
    zstart = ((pstart + counts) // MOE_ROW_Q).astype(I32)
    npad = jnp.where(nb * MOE_TM > counts, bend * (MOE_TM // MOE_ROW_Q) - zstart, 0).astype(I32)
    blocks = tuple(a.astype(I32) for a in (bc, be, bflags, b_nxt_e))
    return dest, zstart, npad, nused.astype(I32).reshape(1), blocks, items


def kernel(x, norm1_g, w_in, b_gates, w_pool_out, pool_scale, sgu_norm_g, w_spatial, b_spatial,
           w_sgu_out, w_out, norm2_g, w_router, b_router, w_gate, b_gate, w_up, b_up, w_down,
           b_down, normf_g):
    bsz, seq, d = x.shape
    n = bsz * seq
    groups, gd, _ = w_pool_out.shape
    pool_w = groups * gd
    sgu_w = w_sgu_out.shape[0]
    n_exp = w_router.shape[1]
    xf = x.reshape(n, d)

    h, pooled = _inproj_pool(xf, norm1_g, w_in[:, :pool_w].astype(BF16), groups, gd, seq)
    s = _inproj_sgu(h, w_in, pool_w, pool_w + sgu_w, sgu_w, sgu_norm_g, w_spatial, b_spatial)
    m = _mix(h, w_in, pool_w + 2 * sgu_w, b_gates, pooled, w_pool_out.astype(BF16), pool_scale,
             s, w_sgu_out)
    x1 = _outproj(m, w_out, xf)

    h2, topi, rank, topw, counts = _router(x1, norm2_g, w_router, b_router)

    n_blocks = (n * TOP_K) // MOE_TM + n_exp
    n_ftiles = w_gate.shape[2] // MOE_TF
    dest, zstart, npad, nused, blocks, items = _routing_tables(
        counts[:, 0], topi, rank, n_blocks, n_ftiles)

    xs = _dispatch(zstart, npad, nused, dest, h2, n_blocks * MOE_TM)
    hm = _expert_up(items, xs, w_gate, w_up, b_gate, b_up)
    y = _expert_down(*blocks, hm, w_down, b_down)
    assert COMB_TM == TOK_TM
    out = _combine(dest, x1, topw.T, normf_g, y)
    return out.reshape(bsz, seq, d)
```
